```python
import jax, jax.numpy as jnp
from jax import lax
import numpy as np

D_MODEL = 1024
BATCH = 8
SEQ = 8192
DEPTH = 1
DEC_BATCH = 128
DEC_SEQ = 1
PAST_LEN = 8192
PAGE_SIZE = 128

HEAD_DIM = 64
D_MIX = D_MODEL
NSA_HEADS = D_MIX // (2 * HEAD_DIM)
NSA_KV_HEADS = 2
NSA_HPG = NSA_HEADS // NSA_KV_HEADS
RWKV_HEADS = D_MIX // (4 * HEAD_DIM)
MEM_HEADS = D_MIX // (4 * HEAD_DIM)
MEM_LEN = 256
NSA_W = NSA_HEADS * HEAD_DIM
KV_W = NSA_KV_HEADS * HEAD_DIM
RWKV_W = RWKV_HEADS * HEAD_DIM
MEM_W = MEM_HEADS * HEAD_DIM
CMP_BLK = 32
CMP_STRIDE = 16
CMP_HID = HEAD_DIM
SEL_BLK = 64
N_SEL = 16
WINDOW = 512
QBLK = 128
RWKV_LORA_W = 64
RWKV_LORA_A = 64
RWKV_SHIFT_W = 3 * RWKV_W + RWKV_LORA_W + RWKV_LORA_A
ROPE_THETA = 10000.0
RMS_EPS = 1e-6
GN_EPS = 64e-5
SCALE = HEAD_DIM ** -0.5
IN_SIZES = (NSA_W, 6 * KV_W, 3 * NSA_HEADS, NSA_W, RWKV_SHIFT_W, RWKV_W, MEM_W, MEM_W)
N_IN = sum(IN_SIZES)

kernel_name = 'nsa_rwkv7_memory_hybrid_step'


def rms_norm(x, g):
    xf = x.astype(jnp.float32)
    y = xf * lax.rsqrt(jnp.mean(xf * xf, axis=-1, keepdims=True) + RMS_EPS)
    return (y * g.astype(jnp.float32)).astype(x.dtype)


def rope(x, pos):
    half = HEAD_DIM // 2
    inv = ROPE_THETA ** (-2.0 * jnp.arange(half, dtype=jnp.float32) / HEAD_DIM)
    ang = pos.astype(jnp.float32)[:, None] * inv[None, :]
    cos = jnp.cos(ang)[:, None, :]
    sin = jnp.sin(ang)[:, None, :]
    xf = x.astype(jnp.float32)
    x1, x2 = xf[..., :half], xf[..., half:]
    return jnp.concatenate([x1 * cos - x2 * sin, x2 * cos + x1 * sin], axis=-1).astype(x.dtype)


def masked_softmax(s, mask):
    s = jnp.where(mask, s.astype(jnp.float32), -jnp.inf)
    m = jnp.max(s, axis=-1, keepdims=True)
    m = jnp.where(jnp.isfinite(m), m, 0.0)
    p = jnp.exp(s - m)
    return p / jnp.maximum(jnp.sum(p, axis=-1, keepdims=True), 1e-30)


def split_in(proj):
    cuts = [int(c) for c in np.cumsum(IN_SIZES)[:-1]]
    return jnp.split(proj, cuts, axis=-1)


def compress(rows2, pe, w1, b1, w2, b2):
    t = rows2.shape[1]
    n_ch = t // CMP_STRIDE
    ch = rows2[:, :n_ch * CMP_STRIDE].reshape(2, n_ch, CMP_STRIDE, NSA_KV_HEADS, HEAD_DIM)
    lo = jnp.einsum('ecpgd,epdh->ecgh', ch + pe[:, None, :CMP_STRIDE, None, :], w1[:, :CMP_STRIDE])
    hi = jnp.einsum('ecpgd,epdh->ecgh', ch + pe[:, None, CMP_STRIDE:, None, :], w1[:, CMP_STRIDE:])
    hid = jax.nn.silu(lo[:, :-1] + hi[:, 1:] + b1[:, None, None, :])
    return jnp.einsum('ecgh,ehd->ecgd', hid, w2) + b2[:, None, None, :]


def nsa_context(rows4, cmp_pe, cmp_w1, cmp_b1, cmp_w2, cmp_b2, ck_norm):
    t = rows4.shape[0]
    cmp2 = compress(jnp.moveaxis(rows4[:, :2], 1, 0), cmp_pe, cmp_w1, cmp_b1, cmp_w2, cmp_b2)
    n_cb = cmp2.shape[1]
    cend = jnp.arange(n_cb) * CMP_STRIDE + CMP_BLK - 1
    ck = rope(rms_norm(cmp2[0], ck_norm), cend)
    cv = cmp2[1]
    n_sb = -(-t // SEL_BLK)
    sel = jnp.pad(rows4[:, 2:], ((0, n_sb * SEL_BLK - t), (0, 0), (0, 0), (0, 0)))
    sel = sel.reshape(n_sb, SEL_BLK, 2, NSA_KV_HEADS, HEAD_DIM).transpose(2, 3, 0, 1, 4)
    cstart = jnp.arange(n_cb) * CMP_STRIDE
    sstart = jnp.arange(n_sb) * SEL_BLK
    ov = jnp.clip(jnp.minimum(cstart[:, None] + CMP_BLK, sstart[None, :] + SEL_BLK)
                  - jnp.maximum(cstart[:, None], sstart[None, :]), 0)
    ov = ov.astype(jnp.float32) / CMP_BLK
    return ck, cv, cend, sel[0], sel[1], ov


def cmp_sel_attend(q, qpos, ck, cv, cend, ksb, vsb, ov):
    nq = q.shape[0]
    qg = q.reshape(nq, NSA_KV_HEADS, NSA_HPG, HEAD_DIM)
    s = jnp.einsum('qghd,cgd->qghc', qg, ck) * SCALE
    valid = cend[None, :] <= qpos[:, None]
    p = masked_softmax(s, valid[:, None, None, :])
    o_c = jnp.einsum('qghc,cgd->qghd', p.astype(cv.dtype), cv)
    imp = jnp.einsum('qghc,cj->qgj', p, ov)
    n_sb = ksb.shape[1]
    jj = jnp.arange(n_sb)
    cur = qpos // SEL_BLK
    forced = (jj[None, :] == 0) | (jj[None, :] == cur[:, None]) | (jj[None, :] == cur[:, None] - 1)
    causal = jj[None, :] * SEL_BLK <= qpos[:, None]
    score = jnp.where(causal[:, None, :], jnp.where(forced[:, None, :], jnp.inf, imp), -jnp.inf)
    n_sel = min(N_SEL, n_sb)
    _, idx = lax.top_k(score, n_sel)
    gidx = jnp.arange(NSA_KV_HEADS)[None, :, None]
    kg = ksb[gidx, idx]
    vg = vsb[gidx, idx]
    ss = jnp.einsum('qghd,qgnsd->qghns', qg, kg) * SCALE
    kp = idx[..., None] * SEL_BLK + jnp.arange(SEL_BLK)
    km = kp <= qpos[:, None, None, None]
    ps = masked_softmax(ss.reshape(nq, NSA_KV_HEADS, NSA_HPG, n_sel * SEL_BLK),
                        km.reshape(nq, NSA_KV_HEADS, 1, n_sel * SEL_BLK))
    o_s = jnp.einsum('qghk,qgkd->qghd', ps.astype(vg.dtype),
                     vg.reshape(nq, NSA_KV_HEADS, n_sel * SEL_BLK, HEAD_DIM))
    return o_c.reshape(nq, NSA_HEADS, HEAD_DIM), o_s.reshape(nq, NSA_HEADS, HEAD_DIM)


def window_attend(q, qpos, kw, vw, kpos):
    nq = q.shape[0]
    qg = q.reshape(nq, NSA_KV_HEADS, NSA_HPG, HEAD_DIM)
    s = jnp.einsum('qghd,kgd->qghk', qg, kw) * SCALE
    mask = (kpos[None, :] <= qpos[:, None]) & (kpos[None, :] > qpos[:, None] - WINDOW) & (kpos[None, :] >= 0)
    p = masked_softmax(s, mask[:, None, None, :])
    o = jnp.einsum('qghk,kgd->qghd', p.astype(vw.dtype), vw)
    return o.reshape(nq, NSA_HEADS, HEAD_DIM)


def gate_mix(g, o_c, o_s, o_w):
    return g[..., 0:1] * o_c + g[..., 1:2] * o_s + g[..., 2:3] * o_w


def nsa_prompt_seq(q, rows4, rows_w, gates, cmp_pe, cmp_w1, cmp_b1, cmp_w2, cmp_b2, ck_norm):
    t = q.shape[0]
    ck, cv, cend, ksb, vsb, ov = nsa_context(rows4, cmp_pe, cmp_w1, cmp_b1, cmp_w2, cmp_b2, ck_norm)
    wpad = jnp.pad(rows_w, ((WINDOW, 0), (0, 0), (0, 0), (0, 0)))

    def one_block(start):
        qb = lax.dynamic_slice_in_dim(q, start, QBLK)
        gb = lax.dynamic_slice_in_dim(gates, start, QBLK)
        qpos = start + jnp.arange(QBLK)
        o_c, o_s = cmp_sel_attend(qb, qpos, ck, cv, cend, ksb, vsb, ov)
        wb = lax.dynamic_slice_in_dim(wpad, start, QBLK + WINDOW)
        kpos = start - WINDOW + jnp.arange(QBLK + WINDOW)
        o_w = window_attend(qb, qpos, wb[:, 0], wb[:, 1], kpos)
        return gate_mix(gb, o_c, o_s, o_w)

    out = lax.map(one_block, jnp.arange(t // QBLK) * QBLK)
    return out.reshape(t, NSA_HEADS, HEAD_DIM)


def nsa_decode_seq(q, pt_row, rows4_new, rows_w_new, win_buf, gates, pool,
                   cmp_pe, cmp_w1, cmp_b1, cmp_w2, cmp_b2, ck_norm):
    past = pool[pt_row].reshape(-1, 4, NSA_KV_HEADS, HEAD_DIM)
    p_len = past.shape[0]
    s_len = q.shape[0]
    rows4 = jnp.concatenate([past, rows4_new], axis=0)
    ck, cv, cend, ksb, vsb, ov = nsa_context(rows4, cmp_pe, cmp_w1, cmp_b1, cmp_w2, cmp_b2, ck_norm)
    qpos = p_len + jnp.arange(s_len)
    o_c, o_s = cmp_sel_attend(q, qpos, ck, cv, cend, ksb, vsb, ov)
    nb = win_buf.shape[0]
    kw = jnp.concatenate([win_buf, rows_w_new], axis=0)
    kpos = p_len - nb + jnp.arange(nb + s_len)
    o_w = window_attend(q, qpos, kw[:, 0], kw[:, 1], kpos)
    return gate_mix(gates, o_c, o_s, o_w), kw[s_len:]


def nsa_project(q_raw, kv_raw, pos, q_norm, k_norm):
    b, t = q_raw.shape[:2]
    q = rope(rms_norm(q_raw.reshape(b, t, NSA_HEADS, HEAD_DIM), q_norm), pos)
    kv = kv_raw.reshape(b, t, 6, NSA_KV_HEADS, HEAD_DIM)
    k_sel = rope(rms_norm(kv[:, :, 2], k_norm[1]), pos)
    k_win = rope(rms_norm(kv[:, :, 4], k_norm[2]), pos)
    rows4 = jnp.stack([kv[:, :, 0], kv[:, :, 1], k_sel, kv[:, :, 3]], axis=2)
    rows_w = jnp.stack([k_win, kv[:, :, 5]], axis=2)
    return q, rows4, rows_w


def rwkv7_mix(u_cur, prev_row, s0, mu, w0, w2, a0, a2, k_k, k_a, r_k, ln_w, ln_b):
    b, t = u_cur.shape[:2]
    prev = jnp.concatenate([prev_row[:, None], u_cur[:, :-1]], axis=1)
    u = u_cur + (prev - u_cur) * mu
    r, k, v, wl, al = jnp.split(u, [RWKV_W, 2 * RWKV_W, 3 * RWKV_W, 3 * RWKV_W + RWKV_LORA_W], axis=-1)
    w = -jax.nn.softplus(-(w0 + jnp.tanh(wl) @ w2)) - 0.5
    decay = jnp.exp(-jnp.exp(w.astype(jnp.float32)))
    a = jax.nn.sigmoid(a0 + al @ a2)

    def heads(z):
        return z.reshape(b, t, RWKV_HEADS, HEAD_DIM).astype(jnp.float32)

    r, k, v, a, decay = heads(r), heads(k), heads(v), heads(a), heads(decay)
    kk = k * k_k.reshape(RWKV_HEADS, HEAD_DIM).astype(jnp.float32)
    kk = kk / jnp.maximum(jnp.sqrt(jnp.sum(kk * kk, axis=-1, keepdims=True)), 1e-12)
    k = k * (1.0 + (a - 1.0) * k_a.reshape(RWKV_HEADS, HEAD_DIM).astype(jnp.float32))

    def step(s, inp):
        r_t, k_t, v_t, w_t, kk_t, b_t = inp
        sa = jnp.einsum('bhvk,bhk->bhv', s, -kk_t)
        s = s * w_t[:, :, None, :] + sa[..., None] * b_t[:, :, None, :] + v_t[..., None] * k_t[:, :, None, :]
        return s, jnp.einsum('bhvk,bhk->bhv', s, r_t)

    xs = tuple(jnp.moveaxis(z, 1, 0) for z in (r, k, v, decay, kk, kk * a))
    s_fin, ys = lax.scan(step, s0.astype(jnp.float32), xs)
    y = jnp.moveaxis(ys, 0, 1)
    mean = jnp.mean(y, axis=-1, keepdims=True)
    var = jnp.mean(jnp.square(y - mean), axis=-1, keepdims=True)
    y = ((y - mean) * lax.rsqrt(var + GN_EPS)).reshape(b, t, RWKV_W)
    y = y * ln_w.astype(jnp.float32) + ln_b.astype(jnp.float32)
    bonus = jnp.sum(r * k * r_k.astype(jnp.float32), axis=-1, keepdims=True) * v
    y = y + bonus.reshape(b, t, RWKV_W)
    return y.astype(u_cur.dtype), u_cur[:, -1], s_fin.astype(s0.dtype)


def mem_kv(mem, g, w_kv, k_norm):
    b = mem.shape[0]
    kv = (rms_norm(mem, g) @ w_kv).reshape(b, mem.shape[1], 2, MEM_HEADS, HEAD_DIM)
    return jnp.stack([rms_norm(kv[:, :, 0], k_norm), kv[:, :, 1]], axis=2)


def mem_attend(q_raw, mkv, q_norm):
    b, t = q_raw.shape[:2]
    q = rms_norm(q_raw.reshape(b, t, MEM_HEADS, HEAD_DIM), q_norm)
    s = jnp.einsum('bthd,bmhd->bhtm', q, mkv[:, :, 0]) * SCALE
    p = jax.nn.softmax(s.astype(jnp.float32), axis=-1)
    return jnp.einsum('bhtm,bmhd->bthd', p.astype(mkv.dtype), mkv[:, :, 1])


def mixer_out(o_nsa, z_nsa, o_rw, z_rw, o_mem, z_mem, w_out):
    b, t = z_nsa.shape[:2]
    cat = jnp.concatenate([o_nsa.reshape(b, t, NSA_W) * jax.nn.silu(z_nsa),
                           o_rw * jax.nn.silu(z_rw),
                           o_mem.reshape(b, t, MEM_W) * jax.nn.silu(z_mem)], axis=-1)
    return cat @ w_out


def setup_inputs(seed: int = 0) -> dict:
    key = jax.random.key(seed)
    ks = iter(jax.random.split(key, 48))

    def nrm(shape, scale):
        return scale * jax.random.normal(next(ks), shape, jnp.float32)

    def gain(shape):
        return 1.0 + nrm(shape, 0.05)

    n_pages = PAST_LEN // PAGE_SIZE
    n_used = DEC_BATCH * n_pages
    n_pool = n_used + n_used // 4
    win_buf = min(WINDOW, PAST_LEN)
    page_table = jax.random.permutation(next(ks), n_pool)[:n_used].reshape(DEC_BATCH, n_pages).astype(jnp.int32)
    return {
        'x_prompt': nrm((BATCH, SEQ, D_MODEL), 1.0),
        'x_sample': nrm((DEC_BATCH, DEC_SEQ, D_MODEL), 1.0),
        'mem_prompt': nrm((BATCH, MEM_LEN, D_MODEL), 1.0),
        'cache_nsa': nrm((DEPTH, n_pool, PAGE_SIZE, 4, NSA_KV_HEADS, HEAD_DIM), 1.0),
        'cache_win': nrm((DEPTH, DEC_BATCH, win_buf, 2, NSA_KV_HEADS, HEAD_DIM), 1.0),
        'cache_mem': nrm((DEPTH, DEC_BATCH, MEM_LEN, 2, MEM_HEADS, HEAD_DIM), 1.0),
        'state_rwkv_shift': nrm((DEPTH, DEC_BATCH, RWKV_SHIFT_W), 1.0),
        'state_rwkv_wkv': nrm((DEPTH, DEC_BATCH, RWKV_HEADS, HEAD_DIM, HEAD_DIM), 0.3),
        'page_table': page_table,
        'ln_g': gain((DEPTH, D_MODEL)),
        'w_in': nrm((DEPTH, D_MODEL, N_IN), D_MODEL ** -0.5),
        'nsa_q_norm': gain((DEPTH, HEAD_DIM)),
        'nsa_k_norm': gain((DEPTH, 3, HEAD_DIM)),
        'cmp_pe': nrm((DEPTH, 2, CMP_BLK, HEAD_DIM), 0.1),
        'cmp_w1': nrm((DEPTH, 2, CMP_BLK, HEAD_DIM, CMP_HID), (CMP_BLK * HEAD_DIM) ** -0.5),
        'cmp_b1': nrm((DEPTH, 2, CMP_HID), 0.02),
        'cmp_w2': nrm((DEPTH, 2, CMP_HID, HEAD_DIM), CMP_HID ** -0.5),
        'cmp_b2': nrm((DEPTH, 2, HEAD_DIM), 0.02),
        'rwkv_mu': jax.random.uniform(next(ks), (DEPTH, RWKV_SHIFT_W), jnp.float32),
        'rwkv_w0': jax.random.uniform(next(ks), (DEPTH, RWKV_W), jnp.float32, -6.0, 0.0),
        'rwkv_w2': nrm((DEPTH, RWKV_LORA_W, RWKV_W), 0.5 * RWKV_LORA_W ** -0.5),
        'rwkv_a0': nrm((DEPTH, RWKV_W), 0.1),
        'rwkv_a2': nrm((DEPTH, RWKV_LORA_A, RWKV_W), 0.5 * RWKV_LORA_A ** -0.5),
        'rwkv_k_k': 0.85 + nrm((DEPTH, RWKV_W), 0.05),
        'rwkv_k_a': gain((DEPTH, RWKV_W)),
        'rwkv_r_k': nrm((DEPTH, RWKV_HEADS, HEAD_DIM), 0.1),
        'rwkv_ln_w': gain((DEPTH, RWKV_W)),
        'rwkv_ln_b': nrm((DEPTH, RWKV_W), 0.02),
        'mem_norm_g': gain((DEPTH, D_MODEL)),
        'w_mem_kv': nrm((DEPTH, D_MODEL, 2 * MEM_W), D_MODEL ** -0.5),
        'mem_q_norm': gain((DEPTH, HEAD_DIM)),
        'mem_k_norm': gain((DEPTH, HEAD_DIM)),
        'w_out': nrm((DEPTH, D_MIX, D_MODEL), D_MIX ** -0.5),
    }


def reference(x_prompt, x_sample, mem_prompt, cache_nsa, cache_win, cache_mem,
              state_rwkv_shift, state_rwkv_wkv, page_table,
              ln_g, w_in, nsa_q_norm, nsa_k_norm, cmp_pe, cmp_w1, cmp_b1, cmp_w2, cmp_b2,
              rwkv_mu, rwkv_w0, rwkv_w2, rwkv_a0, rwkv_a2, rwkv_k_k, rwkv_k_a, rwkv_r_k,
              rwkv_ln_w, rwkv_ln_b, mem_norm_g, w_mem_kv, mem_q_norm, mem_k_norm, w_out):
    bp, tp = x_prompt.shape[:2]
    bs, ts = x_sample.shape[:2]
    pos_p = jnp.arange(tp)
    pos_s = PAST_LEN + jnp.arange(ts)
    win_p = min(WINDOW, tp)
    hp, hs = x_prompt, x_sample
    nsa_p, win_pl, shift_pl, wkv_pl, memkv_pl = [], [], [], [], []
    nsa_sl, win_sl, shift_sl, wkv_sl = [], [], [], []
    for l in range(DEPTH):
        cmp_params = (cmp_pe[l], cmp_w1[l], cmp_b1[l], cmp_w2[l], cmp_b2[l], nsa_k_norm[l, 0])
        rwkv_params = (rwkv_mu[l], rwkv_w0[l], rwkv_w2[l], rwkv_a0[l], rwkv_a2[l], rwkv_k_k[l],
                       rwkv_k_a[l], rwkv_r_k[l], rwkv_ln_w[l], rwkv_ln_b[l])

        q_raw, kv_raw, g_raw, z_nsa, u_rw, z_rw, q_mem, z_mem = split_in(rms_norm(hp, ln_g[l]) @ w_in[l])
        q, rows4, rows_w = nsa_project(q_raw, kv_raw, pos_p, nsa_q_norm[l], nsa_k_norm[l])
        gates = jax.nn.sigmoid(g_raw).reshape(bp, tp, NSA_HEADS, 3)
        o_nsa = lax.map(lambda a: nsa_prompt_seq(*a, *cmp_params), (q, rows4, rows_w, gates))
        o_rw, sh_p, st_p = rwkv7_mix(u_rw, jnp.zeros((bp, RWKV_SHIFT_W), u_rw.dtype),
                                     jnp.zeros((bp, RWKV_HEADS, HEAD_DIM, HEAD_DIM), hp.dtype), *rwkv_params)
        mkv_p = mem_kv(mem_prompt, mem_norm_g[l], w_mem_kv[l], mem_k_norm[l])
        o_mem = mem_attend(q_mem, mkv_p, mem_q_norm[l])
        hp = hp + mixer_out(o_nsa, z_nsa, o_rw, z_rw, o_mem, z_mem, w_out[l])
        nsa_p.append(rows4)
        win_pl.append(rows_w[:, tp - win_p:])
        shift_pl.append(sh_p)
        wkv_pl.append(st_p)
        memkv_pl.append(mkv_p)

        q_raw, kv_raw, g_raw, z_nsa, u_rw, z_rw, q_mem, z_mem = split_in(rms_norm(hs, ln_g[l]) @ w_in[l])
        q, rows4, rows_w = nsa_project(q_raw, kv_raw, pos_s, nsa_q_norm[l], nsa_k_norm[l])
        gates = jax.nn.sigmoid(g_raw).reshape(bs, ts, NSA_HEADS, 3)
        pool = cache_nsa[l]
        o_nsa, win_new = lax.map(lambda a: nsa_decode_seq(*a, pool, *cmp_params),
                                 (q, page_table, rows4, rows_w, cache_win[l], gates))
        o_rw, sh_s, st_s = rwkv7_mix(u_rw, state_rwkv_shift[l], state_rwkv_wkv[l], *rwkv_params)
        o_mem = mem_attend(q_mem, cache_mem[l], mem_q_norm[l])
        hs = hs + mixer_out(o_nsa, z_nsa, o_rw, z_rw, o_mem, z_mem, w_out[l])
        nsa_sl.append(rows4)
        win_sl.append(win_new)
        shift_sl.append(sh_s)
        wkv_sl.append(st_s)

    nsa_rows_prompt = jnp.stack(nsa_p)
    win_prompt = jnp.stack(win_pl)
    shift_prompt = jnp.stack(shift_pl)
    wkv_prompt = jnp.stack(wkv_pl)
    mem_kv_prompt = jnp.stack(memkv_pl)
    nsa_rows_sample = jnp.stack(nsa_sl)
    win_sample = jnp.stack(win_sl)
    shift_sample = jnp.stack(shift_sl)
    wkv_sample = jnp.stack(wkv_sl)
    return (hp, hs, nsa_rows_prompt, win_prompt, shift_prompt, wkv_prompt, mem_kv_prompt,
            nsa_rows_sample, win_sample, shift_sample, wkv_sample)
```

```python
import functools

import numpy as np
import jax
import jax.numpy as jnp
from jax import lax
from jax.experimental import pallas as pl
from jax.experimental.pallas import tpu as pltpu

F32 = jnp.float32
BF16 = jnp.bfloat16

HEAD_DIM = 64
NSA_HEADS = 8
NSA_KV_HEADS = 2
NSA_HPG = 4
RWKV_HEADS = 4
MEM_HEADS = 4
NSA_W = 512
RWKV_W = 256
MEM_W = 256
CMP_BLK = 32
CMP_STRIDE = 16
SEL_BLK = 64
N_SEL = 16
WINDOW = 512
PAGE_SIZE = 128
RWKV_SHIFT_W = 896
ROPE_THETA = 10000.0
RMS_EPS = 1e-6
GN_EPS = 64e-5
SCALE = HEAD_DIM ** -0.5

LANES = 128
QBLK = 128
KTILE = 512
NEG = -1e30

C_Q = 0
C_KV = 1024
C_Z = 1792
C_U = 2816
C_QM = 3712
C_G = 4224
N_INP = 4352

VMEM_LIMIT = 56 * 1024 * 1024


def _cp(sem):
    return pltpu.CompilerParams(dimension_semantics=sem, vmem_limit_bytes=VMEM_LIMIT)


def _nt(a, b):
    return lax.dot_general(a, b, (((1,), (1,)), ((), ())), preferred_element_type=F32)


def _nn(a, b):
    return jnp.dot(a, b, preferred_element_type=F32)


def _split2(x):
    hi = x.astype(BF16)
    lo = (x - hi.astype(F32)).astype(BF16)
    return hi, lo


def _split3(x):
    hi = x.astype(BF16)
    r = x - hi.astype(F32)
    mid = r.astype(BF16)
    lo = (r - mid.astype(F32)).astype(BF16)
    return hi, mid, lo


def _nn_x(a, b):
    ah, al = _split2(a)
    bh, bl = _split2(b)
    return _nn(ah, bh) + _nn(ah, bl) + _nn(al, bh)


def _nt_x(a, b):
    ah, al = _split2(a)
    bh, bl = _split2(b)
    return _nt(ah, bh) + _nt(ah, bl) + _nt(al, bh)


def _segsum(x, bd):
    cols = []
    for c in range(x.shape[1] // LANES):
        hi, lo = _split2(x[:, c * LANES:(c + 1) * LANES])
        cols.append(_nn(hi, bd) + _nn(lo, bd))
    return cols[0] if len(cols) == 1 else jnp.concatenate(cols, axis=1)


def _rot_half(x):
    lane = lax.broadcasted_iota(jnp.int32, x.shape, 1)
    up = pltpu.roll(x, 96, axis=1)
    dn = pltpu.roll(x, 32, axis=1)
    return jnp.where((lane & 63) < 32, up, dn)


def _rope_cols(v, cosf, sinf):
    cols = []
    for c in range(v.shape[1] // LANES):
        xc = v[:, c * LANES:(c + 1) * LANES]
        cols.append(xc * cosf + _rot_half(xc) * sinf)
    return cols[0] if len(cols) == 1 else jnp.concatenate(cols, axis=1)


def _head_rms(v, gain, bd):
    ss = _segsum(v * v, bd)
    return v * lax.rsqrt(ss * (1.0 / HEAD_DIM) + RMS_EPS) * gain


def _sigmoid(x):
    return 1.0 / (1.0 + jnp.exp(-x))


def _softmax_cols(s, mask):
    sm = jnp.where(mask, s, NEG)
    m = jnp.max(sm, axis=0, keepdims=True)
    p = jnp.where(mask, jnp.exp(sm - m), 0.0)
    l = jnp.sum(p, axis=0, keepdims=True)
    return p * (1.0 / jnp.maximum(l, 1e-30))


def _full(shape):
    nd = len(shape)
    return pl.BlockSpec(shape, lambda *_: (0,) * nd)


def _proj_body(x_ref, lng_ref, w_ref, cos_ref, sin_ref, qn_ref, kn_ref, mqn_ref, bd_ref,
               q_out, rows4_out, rowsw_out, kvb_out, zs_out, u_out, qm_out, g_out):
    x = x_ref[...]
    ms = jnp.mean(x * x, axis=-1, keepdims=True)
    xn = (x * lax.rsqrt(ms + RMS_EPS) * lng_ref[...]).astype(BF16)
    bd = bd_ref[...]
    cosf = cos_ref[...]
    sinf = sin_ref[...]

    def seg(a, b):
        return jnp.dot(xn, w_ref[:, a:b], preferred_element_type=F32)

    q = _rope_cols(_head_rms(seg(C_Q, C_KV), qn_ref[...], bd), cosf, sinf) * SCALE
    q_out[...] = q.astype(BF16)

    kv = seg(C_KV, C_Z)
    kn = kn_ref[...]
    ksel = _rope_cols(_head_rms(kv[:, 256:384], kn[1:2], bd), cosf, sinf)
    kwin = _rope_cols(_head_rms(kv[:, 512:640], kn[2:3], bd), cosf, sinf)
    rows4_out[:, 0:256] = kv[:, 0:256]
    rows4_out[:, 256:384] = ksel
    rows4_out[:, 384:512] = kv[:, 384:512]
    rowsw_out[:, 0:128] = kwin
    rowsw_out[:, 128:256] = kv[:, 640:768]
    kvb_out[:, 0:128] = ksel.astype(BF16)
    kvb_out[:, 128:256] = kv[:, 384:512].astype(BF16)
    kvb_out[:, 256:384] = kwin.astype(BF16)
    kvb_out[:, 384:512] = kv[:, 640:768].astype(BF16)

    z = seg(C_Z, C_U)
    zs_out[...] = z * _sigmoid(z)
    u_out[...] = seg(C_U, C_QM)
    qm = _head_rms(seg(C_QM, C_G), mqn_ref[...], bd) * SCALE
    qm_out[...] = qm.astype(BF16)
    g_out[...] = _sigmoid(seg(C_G, N_INP))


def _project(x2d, lng, w_p, cos_t, sin_t, n_tab_blocks, qn, kn, mqn, bd, tb):
    n, d = x2d.shape
    grid = (n // tb,)
    row = lambda i: (i, 0)
    tab = lambda i: (i % n_tab_blocks, 0)
    outs = [
        jax.ShapeDtypeStruct((n, 1024), BF16),
        jax.ShapeDtypeStruct((n, 512), F32),
        jax.ShapeDtypeStruct((n, 256), F32),
        jax.ShapeDtypeStruct((n, 512), BF16),
        jax.ShapeDtypeStruct((n, 1024), F32),
        jax.ShapeDtypeStruct((n, RWKV_SHIFT_W), F32),
        jax.ShapeDtypeStruct((n, 512), BF16),
        jax.ShapeDtypeStruct((n, 128), F32),
    ]
    return pl.pallas_call(
        _proj_body,
        grid=grid,
        in_specs=[
            pl.BlockSpec((tb, d), row),
            _full((1, d)),
            _full((d, N_INP)),
            pl.BlockSpec((tb, LANES), tab),
            pl.BlockSpec((tb, LANES), tab),
            _full((1, 1024)),
            _full((3, LANES)),
            _full((1, 512)),
            _full((LANES, LANES)),
        ],
        out_specs=[pl.BlockSpec((tb, o.shape[1]), row) for o in outs],
        out_shape=outs,
        compiler_params=_cp(("arbitrary",)),
        name="proj",
    )(x2d, lng, w_p, cos_t, sin_t, qn, kn, mqn, bd)


def _compress_compute(xk_ref, xv_ref, n_ch, pelo_ref, pehi_ref, w1lo_ref, w1hi_ref, b1_ref, w2_ref, b2_ref,
                      ckn_ref, cos_ref, sin_ref, bd_ref):
    lo = jnp.zeros((n_ch, 256), F32)
    hi = jnp.zeros((n_ch, 256), F32)
    for p in range(CMP_STRIDE):
        xp = jnp.concatenate([xk_ref[pl.ds(p, n_ch, stride=CMP_STRIDE), :],
                              xv_ref[pl.ds(p, n_ch, stride=CMP_STRIDE), :]], axis=1)
        lo = lo + _nn((xp + pelo_ref[p:p + 1, :]).astype(BF16), w1lo_ref[p])
        hi = hi + _nn((xp + pehi_ref[p:p + 1, :]).astype(BF16), w1hi_ref[p])
    his = pltpu.roll(hi, n_ch - 1, axis=0)
    pre = lo + his + b1_ref[...]
    hid = pre * _sigmoid(pre)
    out2 = _nn(hid.astype(BF16), w2_ref[...]) + b2_ref[...]
    ck = _head_rms(out2[:, 0:128], ckn_ref[...], bd_ref[...])
    ck = _rope_cols(ck, cos_ref[...], sin_ref[...])
    return ck, out2[:, 128:256]


def _compress_body(xk_ref, xv_ref, pelo_ref, pehi_ref, w1lo_ref, w1hi_ref, b1_ref, w2_ref, b2_ref,
                   ckn_ref, cos_ref, sin_ref, bd_ref, ck_out, cv_out, *, n_ch):
    ck, cv = _compress_compute(xk_ref, xv_ref, n_ch, pelo_ref, pehi_ref, w1lo_ref, w1hi_ref, b1_ref, w2_ref,
                               b2_ref, ckn_ref, cos_ref, sin_ref, bd_ref)
    ck_out[0] = ck.astype(BF16)
    cv_out[0] = cv.astype(BF16)


def _cmp_specs(n_ch):
    return [
        _full((CMP_STRIDE, 256)), _full((CMP_STRIDE, 256)),
        _full((CMP_STRIDE, 256, 256)), _full((CMP_STRIDE, 256, 256)),
        _full((1, 256)), _full((256, 256)), _full((1, 256)),
        _full((1, LANES)), _full((n_ch, LANES)), _full((n_ch, LANES)), _full((LANES, LANES)),
    ]


def _compress_prompt(rows4, cmp_args, b, t):
    n_ch = t // CMP_STRIDE
    outs = [jax.ShapeDtypeStruct((b, n_ch, LANES), BF16)] * 2
    return pl.pallas_call(
        functools.partial(_compress_body, n_ch=n_ch),
        grid=(b,),
        in_specs=[pl.BlockSpec((t, LANES), lambda i: (i, 0)), pl.BlockSpec((t, LANES), lambda i: (i, 1))]
        + _cmp_specs(n_ch),
        out_specs=[pl.BlockSpec((1, n_ch, LANES), lambda i: (i, 0, 0))] * 2,
        out_shape=outs,
        compiler_params=_cp(("arbitrary",)),
        name="compress_prompt",
    )(rows4, rows4, *cmp_args)


def _select_blocks(score, n_sel):
    jj = lax.broadcasted_iota(jnp.int32, score.shape, 0)
    big = jnp.int32(score.shape[0])
    sel = jnp.zeros(score.shape, F32)
    for _ in range(n_sel):
        mx = jnp.max(score, axis=0, keepdims=True)
        first = jnp.min(jnp.where(score == mx, jj, big), axis=0, keepdims=True)
        pick = jj == first
        sel = jnp.where(pick, 1.0, sel)
        score = jnp.where(pick, -jnp.inf, score)
    return sel


def _nsa_prompt_body(q_ref, g_ref, kvb_ref, ck_ref, cv_ref, ovt_ref, et_ref, o_ref, *, n_cb, n_sb):
    i = pl.program_id(1)
    start = i * QBLK
    q = q_ref[...]
    r0 = lax.broadcasted_iota(jnp.int32, (LANES, LANES), 0)
    r1 = lax.broadcasted_iota(jnp.int32, (LANES, LANES), 1)
    eye = jnp.where(r0 == r1, 1.0, 0.0).astype(BF16)
    gT = g_ref[...].T
    qpos4 = start + (lax.broadcasted_iota(jnp.int32, (1, 4 * QBLK), 1) & (QBLK - 1))
    cvT = _nt(eye, cv_ref[0]).astype(BF16)
    ck = ck_ref[0]
    ovt = ovt_ref[...]
    wk = WINDOW + QBLK
    n_tiles = (start + QBLK + KTILE - 1) // KTILE
    lo_w = pl.multiple_of(jnp.maximum(start - WINDOW, 0), QBLK)

    heads = []
    for g in range(NSA_KV_HEADS):
        qg = jnp.concatenate([q[:, (4 * g + h) * LANES:(4 * g + h + 1) * LANES] for h in range(4)], axis=0)

        sc = _nt(ck, qg)
        cend = lax.broadcasted_iota(jnp.int32, (n_cb, 1), 0) * CMP_STRIDE + (CMP_BLK - 1)
        p = _softmax_cols(sc, cend <= qpos4)
        ocT = _nn(cvT, p.astype(BF16))
        psum = p[:, 0:128] + p[:, 128:256] + p[:, 256:384] + p[:, 384:512]
        ph, plo = _split2(psum)
        imp = _nn(ovt, ph) + _nn(ovt, plo)

        jj = lax.broadcasted_iota(jnp.int32, (n_sb, QBLK), 0)
        qp = start + lax.broadcasted_iota(jnp.int32, (n_sb, QBLK), 1)
        cur = lax.shift_right_logical(qp, 6)
        forced = (jj == 0) | (jj == cur) | (jj == cur - 1)
        causal = jj * SEL_BLK <= qp
        score = jnp.where(causal, jnp.where(forced, jnp.inf, imp), -jnp.inf)
        sel = _select_blocks(score, N_SEL).astype(BF16)

        def tile_step(t, carry):
            m, l, acc = carry
            k0 = pl.multiple_of(t * KTILE, KTILE)
            k = kvb_ref[pl.ds(k0, KTILE), 0:128]
            v = kvb_ref[pl.ds(k0, KTILE), 128:256]
            s = _nt(k, qg)
            mb = _nn(et_ref[t], sel)
            mb4 = jnp.concatenate([mb, mb, mb, mb], axis=1)
            kpos = k0 + lax.broadcasted_iota(jnp.int32, (KTILE, 1), 0)
            msk = (mb4 > 0.5) & (kpos <= qpos4)
            sm = jnp.where(msk, s, NEG)
            mn = jnp.maximum(m, jnp.max(sm, axis=0, keepdims=True))
            alpha = jnp.exp(m - mn)
            pp = jnp.where(msk, jnp.exp(sm - mn), 0.0)
            l = alpha * l + jnp.sum(pp, axis=0, keepdims=True)
            vT = _nt(eye, v).astype(BF16)
            acc = alpha * acc + _nn(vT, pp.astype(BF16))
            return mn, l, acc

        m0 = jnp.full((1, 4 * QBLK), NEG, F32)
        l0 = jnp.zeros((1, 4 * QBLK), F32)
        a0 = jnp.zeros((LANES, 4 * QBLK), F32)
        _, l, acc = lax.fori_loop(0, n_tiles, tile_step, (m0, l0, a0))
        osT = acc * (1.0 / jnp.maximum(l, 1e-30))

        kw = kvb_ref[pl.ds(lo_w, wk), 256:384]
        vw = kvb_ref[pl.ds(lo_w, wk), 384:512]
        sw = _nt(kw, qg)
        kpos = lo_w + lax.broadcasted_iota(jnp.int32, (wk, 1), 0)
        pw = _softmax_cols(sw, (kpos <= qpos4) & (kpos > qpos4 - WINDOW))
        owT = _nn(_nt(eye, vw).astype(BF16), pw.astype(BF16))

        for h in range(4):
            hd = 4 * g + h
            cs = slice(h * QBLK, (h + 1) * QBLK)
            mixT = (gT[3 * hd:3 * hd + 1, :] * ocT[:, cs] + gT[3 * hd + 1:3 * hd + 2, :] * osT[:, cs]
                    + gT[3 * hd + 2:3 * hd + 3, :] * owT[:, cs])
            mix = mixT.T
            if hd % 2 != g:
                mix = pltpu.roll(mix, 64, axis=1)
            heads.append(mix)

    cols = [jnp.where(r1 < 64, heads[2 * c], heads[2 * c + 1]) for c in range(4)]
    o_ref[...] = jnp.concatenate(cols, axis=1)


def _nsa_prompt(q, gates, kvb, ck, cv, ovt, et, b, t):
    nq = t // QBLK
    n_cb = ck.shape[1]
    n_sb = ovt.shape[0]
    return pl.pallas_call(
        functools.partial(_nsa_prompt_body, n_cb=n_cb, n_sb=n_sb),
        grid=(b, nq),
        in_specs=[
            pl.BlockSpec((QBLK, 1024), lambda bi, i: (bi * nq + i, 0)),
            pl.BlockSpec((QBLK, LANES), lambda bi, i: (bi * nq + i, 0)),
            pl.BlockSpec((t, 512), lambda bi, i: (bi, 0)),
            pl.BlockSpec((1, n_cb, LANES), lambda bi, i: (bi, 0, 0)),
            pl.BlockSpec((1, n_cb, LANES), lambda bi, i: (bi, 0, 0)),
            _full(ovt.shape),
            _full(et.shape),
        ],
        out_specs=pl.BlockSpec((QBLK, 512), lambda bi, i: (bi * nq + i, 0)),
        out_shape=jax.ShapeDtypeStruct((b * t, 512), F32),
        compiler_params=_cp(("arbitrary", "arbitrary")),
        name="nsa_prompt",
    )(q, gates, kvb, ck, cv, ovt, et)


def _rwkv_body(u_ref, s0_ref, sh0_ref, mu_ref, w0_ref, w2_ref, a0_ref, a2_ref, kk_ref, ka_ref, rk_ref,
               lnw_ref, lnb_ref, bd_ref, o_ref, sout_ref, s_sc, prev_sc, *, chunk, t_valid, n_dbl):
    c = pl.program_id(1)
    C = chunk

    @pl.when(c == 0)
    def _():
        s_sc[...] = s0_ref[0]
        prev_sc[...] = sh0_ref[0]

    ucur = u_ref[...]
    rowi = lax.broadcasted_iota(jnp.int32, (C, 1), 0)
    prev = jnp.where(rowi == 0, prev_sc[...], pltpu.roll(ucur, 1, axis=0))
    prev_sc[...] = ucur[C - 1:C, :]
    u = ucur + (prev - ucur) * mu_ref[...]
    r = u[:, 0:256]
    k = u[:, 256:512]
    v = u[:, 512:768]
    wl = u[:, 768:832]
    al = u[:, 832:896]
    bd = bd_ref[...]

    ww = w0_ref[...] + _nn_x(jnp.tanh(wl), w2_ref[...])
    nw = -ww
    sp = jnp.maximum(nw, 0.0) + jnp.log(1.0 + jnp.exp(-jnp.abs(nw)))
    logdec = -jnp.exp(-sp - 0.5)
    a = _sigmoid(a0_ref[...] + _nn_x(al, a2_ref[...]))
    kk = k * kk_ref[...]
    kk = kk / jnp.maximum(jnp.sqrt(_segsum(kk * kk, bd)), 1e-12)
    k2 = k * (1.0 + (a - 1.0) * ka_ref[...])
    bb = kk * a
    if t_valid < C:
        live = rowi < t_valid
        logdec = jnp.where(live, logdec, 0.0)
        kk = jnp.where(live, kk, 0.0)
        bb = jnp.where(live, bb, 0.0)
        k2 = jnp.where(live, k2, 0.0)
        v = jnp.where(live, v, 0.0)

    ti = lax.broadcasted_iota(jnp.int32, (C, C), 0)
    ii = lax.broadcasted_iota(jnp.int32, (C, C), 1)
    tri = jnp.where(ti >= ii, 1.0, 0.0).astype(BF16)
    d1, d2, d3 = _split3(logdec)
    lcum = _nn(tri, d1) + _nn(tri, d2) + _nn(tri, d3)
    lprev = lcum - logdec
    lend = lcum[C - 1:C, :]
    e_neg = jnp.exp(-lcum)
    at = -kk * jnp.exp(lprev)
    bt = bb * e_neg
    kt = k2 * e_neg
    rt = r * jnp.exp(lcum)
    e_end = jnp.exp(lend - lcum)
    bg = bb * e_end
    kg = k2 * e_end
    gend = jnp.exp(lend)

    e0 = lax.broadcasted_iota(jnp.int32, (HEAD_DIM, HEAD_DIM), 0)
    e1 = lax.broadcasted_iota(jnp.int32, (HEAD_DIM, HEAD_DIM), 1)
    eye64 = jnp.where(e0 == e1, 1.0, 0.0).astype(BF16)
    eye_c = jnp.where(ti == ii, 1.0, 0.0)
    strict = ti > ii
    incl = ti >= ii

    ys = []
    for h in range(RWKV_HEADS):
        sl = slice(h * HEAD_DIM, (h + 1) * HEAD_DIM)
        s_old = s_sc[h]
        ar = jnp.concatenate([at[:, sl], rt[:, sl]], axis=0)
        bk = jnp.concatenate([bt[:, sl], kt[:, sl]], axis=0)
        vh = v[:, sl]
        mm = _nt_x(ar, bk)
        lab = jnp.where(strict, mm[0:C, 0:C], 0.0)
        lak = jnp.where(strict, mm[0:C, C:2 * C], 0.0)
        mrb = jnp.where(incl, mm[C:2 * C, 0:C], 0.0)
        mrk = jnp.where(incl, mm[C:2 * C, C:2 * C], 0.0)
        ars = _nt_x(ar, s_old)
        rhs = ars[0:C] + _nn_x(lak, vh)
        x = eye_c + lab
        pw = lab
        for _ in range(n_dbl):
            pw = _nn_x(pw, pw)
            x = x + _nn_x(pw, x)
        uu = _nn_x(x, rhs)
        ys.append(ars[C:2 * C] + _nn_x(mrb, uu) + _nn_x(mrk, vh))
        uv = jnp.concatenate([uu, vh], axis=0)
        uvh, uvl = _split2(uv)
        uvt_h = _nt(eye64, uvh).astype(BF16)
        uvt_l = _nt(eye64, uvl).astype(BF16)
        bkg = jnp.concatenate([bg[:, sl], kg[:, sl]], axis=0)
        gh, gl = _split2(bkg)
        s_new = s_old * gend[:, sl] + _nn(uvt_h, gh) + _nn(uvt_h, gl) + _nn(uvt_l, gh)
        s_sc[h] = s_new

    y = jnp.concatenate(ys, axis=1)
    mean = _segsum(y, bd) * (1.0 / HEAD_DIM)
    dy = y - mean
    var = _segsum(dy * dy, bd) * (1.0 / HEAD_DIM)
    yn = dy * lax.rsqrt(var + GN_EPS) * lnw_ref[...] + lnb_ref[...]
    bonus = _segsum(r * k2 * rk_ref[...], bd) * v
    o_ref[...] = yn + bonus

    @pl.when(c == pl.num_programs(1) - 1)
    def _():
        sout_ref[0] = s_sc[...]


def _rwkv(u2d, s0, sh0, rw_args, bd, nb, n_chunks, chunk, t_valid):
    n_dbl = max(int(np.log2(chunk)) - 1, 0)
    outs = [jax.ShapeDtypeStruct((u2d.shape[0], RWKV_W), F32),
            jax.ShapeDtypeStruct((nb, RWKV_HEADS, HEAD_DIM, HEAD_DIM), F32)]
    vec = _full((1, RWKV_W))
    return pl.pallas_call(
        functools.partial(_rwkv_body, chunk=chunk, t_valid=t_valid, n_dbl=n_dbl),
        grid=(nb, n_chunks),
        in_specs=[
            pl.BlockSpec((chunk, RWKV_SHIFT_W), lambda b, c: (b * n_chunks + c, 0)),
            pl.BlockSpec((1, RWKV_HEADS, HEAD_DIM, HEAD_DIM), lambda b, c: (b, 0, 0, 0)),
            pl.BlockSpec((1, 1, RWKV_SHIFT_W), lambda b, c: (b, 0, 0)),
            _full((1, RWKV_SHIFT_W)),
            vec, _full((HEAD_DIM, RWKV_W)), vec, _full((HEAD_DIM, RWKV_W)),
            vec, vec, vec, vec, vec, _full((LANES, LANES)),
        ],
        out_specs=[pl.BlockSpec((chunk, RWKV_W), lambda b, c: (b * n_chunks + c, 0)),
                   pl.BlockSpec((1, RWKV_HEADS, HEAD_DIM, HEAD_DIM), lambda b, c: (b, 0, 0, 0))],
        out_shape=outs,
        scratch_shapes=[pltpu.VMEM((RWKV_HEADS, HEAD_DIM, HEAD_DIM), F32),
                        pltpu.VMEM((1, RWKV_SHIFT_W), F32)],
        compiler_params=_cp(("arbitrary", "arbitrary")),
        name="rwkv",
    )(u2d, s0, sh0, *rw_args, bd)


def _memkv_body(m_ref, g_ref, w_ref, kn_ref, bd_ref, o_ref):
    x = m_ref[0]
    ms = jnp.mean(x * x, axis=-1, keepdims=True)
    xn = (x * lax.rsqrt(ms + RMS_EPS) * g_ref[...]).astype(BF16)
    kv = _nn(xn, w_ref[...])
    o_ref[0, :, 0:256] = _head_rms(kv[:, 0:256], kn_ref[...], bd_ref[...])
    o_ref[0, :, 256:512] = kv[:, 256:512]


def _mem_kv(mem, g, w, kn, bd):
    b, m, d = mem.shape
    return pl.pallas_call(
        _memkv_body,
        grid=(b,),
        in_specs=[pl.BlockSpec((1, m, d), lambda i: (i, 0, 0)), _full((1, d)), _full((d, 512)),
                  _full((1, 256)), _full((LANES, LANES))],
        out_specs=pl.BlockSpec((1, m, 512), lambda i: (i, 0, 0)),
        out_shape=jax.ShapeDtypeStruct((b, m, 512), F32),
        compiler_params=_cp(("arbitrary",)),
        name="mem_kv",
    )(mem, g, w, kn, bd)


def _mem_attn_body(qm_ref, kv_ref, o_ref):
    qm = qm_ref[...]
    kvb = kv_ref[0].astype(BF16)
    r0 = lax.broadcasted_iota(jnp.int32, (LANES, LANES), 0)
    r1 = lax.broadcasted_iota(jnp.int32, (LANES, LANES), 1)
    eye = jnp.where(r0 == r1, 1.0, 0.0).astype(BF16)
    heads = []
    for h in range(MEM_HEADS):
        col = h // 2
        kc = kvb[:, col * LANES:(col + 1) * LANES]
        vc = kvb[:, 256 + col * LANES:256 + (col + 1) * LANES]
        s = _nt(kc, qm[:, h * LANES:(h + 1) * LANES])
        m = jnp.max(s, axis=0, keepdims=True)
        p = jnp.exp(s - m)
        p = p * (1.0 / jnp.sum(p, axis=0, keepdims=True))
        oT = _nn(_nt(eye, vc).astype(BF16), p.astype(BF16))
        heads.append(oT.T)
    lane = lax.broadcasted_iota(jnp.int32, heads[0].shape, 1)
    o_ref[...] = jnp.concatenate([jnp.where(lane < 64, heads[0], heads[1]),
                                  jnp.where(lane < 64, heads[2], heads[3])], axis=1)


def _mem_attn(qm, memkv, nb, rows_per_b, tb):
    n = qm.shape[0]
    nblk = rows_per_b // tb
    m = memkv.shape[1]
    return pl.pallas_call(
        _mem_attn_body,
        grid=(nb, nblk),
        in_specs=[pl.BlockSpec((tb, 512), lambda b, i: (b * nblk + i, 0)),
                  pl.BlockSpec((1, m, 512), lambda b, i: (b, 0, 0))],
        out_specs=pl.BlockSpec((tb, MEM_W), lambda b, i: (b * nblk + i, 0)),
        out_shape=jax.ShapeDtypeStruct((n, MEM_W), F32),
        compiler_params=_cp(("arbitrary", "arbitrary")),
        name="mem_attn",
    )(qm, memkv)


def _out_body(x_ref, on_ref, or_ref, om_ref, zs_ref, w_ref, y_ref):
    zs = zs_ref[...]
    cat = jnp.concatenate([on_ref[...] * zs[:, 0:512], or_ref[...] * zs[:, 512:768],
                           om_ref[...] * zs[:, 768:1024]], axis=1)
    y_ref[...] = x_ref[...] + _nn(cat.astype(BF16), w_ref[...])


def _out_proj(x2d, o_nsa, o_rw, o_mem, zs, w_out, tb):
    n, d = x2d.shape
    row = lambda i: (i, 0)
    return pl.pallas_call(
        _out_body,
        grid=(n // tb,),
        in_specs=[pl.BlockSpec((tb, d), row), pl.BlockSpec((tb, 512), row), pl.BlockSpec((tb, 256), row),
                  pl.BlockSpec((tb, 256), row), pl.BlockSpec((tb, 1024), row), _full((1024, d))],
        out_specs=pl.BlockSpec((tb, d), row),
        out_shape=jax.ShapeDtypeStruct((n, d), F32),
        compiler_params=_cp(("arbitrary",)),
        name="out_proj",
    )(x2d, o_nsa, o_rw, o_mem, zs, w_out)


PAGES_PER_STEP = 8


def _dec_a_body(pt_ref, *refs, n_ch, n_sb, n_sbp, p_len):
    pages = refs[:PAGES_PER_STEP]
    (q_ref, pelo_ref, pehi_ref, w1lo_ref, w1hi_ref, b1_ref, w2_ref, b2_ref, ckn_ref, cos_ref, sin_ref,
     bd_ref, ov_ref, oc_out, idx_out, xk, xv) = refs[PAGES_PER_STEP:]
    j = pl.program_id(1)
    for i in range(PAGES_PER_STEP):
        r0 = pl.multiple_of((j * PAGES_PER_STEP + i) * PAGE_SIZE, PAGE_SIZE)
        xk[pl.ds(r0, PAGE_SIZE), :] = pages[i][0, :, 0:LANES]
        xv[pl.ds(r0, PAGE_SIZE), :] = pages[i][0, :, LANES:2 * LANES]

    @pl.when(j == pl.num_programs(1) - 1)
    def _():
        ck, cv = _compress_compute(xk, xv, n_ch, pelo_ref, pehi_ref, w1lo_ref, w1hi_ref, b1_ref, w2_ref, b2_ref,
                                   ckn_ref, cos_ref, sin_ref, bd_ref)
        ckb = ck.astype(BF16)
        cvb = cv.astype(BF16)
        qrow = q_ref[0]
        rows = lax.broadcasted_iota(jnp.int32, (16, LANES), 0)
        cend = lax.broadcasted_iota(jnp.int32, (1, n_ch), 1) * CMP_STRIDE + (CMP_BLK - 1)
        valid = cend <= p_len
        ocs = []
        imps = []
        for g in range(NSA_KV_HEADS):
            qg = jnp.zeros((16, LANES), F32)
            for h in range(4):
                piece = qrow[:, (4 * g + h) * LANES:(4 * g + h + 1) * LANES].astype(F32)
                qg = jnp.where(rows == h, piece, qg)
            s = _nt(qg.astype(BF16), ckb)
            sm = jnp.where(valid, s, NEG)
            m = jnp.max(sm, axis=1, keepdims=True)
            p = jnp.where(valid, jnp.exp(sm - m), 0.0)
            p = p * (1.0 / jnp.maximum(jnp.sum(p, axis=1, keepdims=True), 1e-30))
            ocs.append(_nn(p.astype(BF16), cvb))
            imps.append(p[0:1] + p[1:2] + p[2:3] + p[3:4])
        oc_out[0] = jnp.concatenate(ocs, axis=1)
        pz = jnp.concatenate(imps + [jnp.zeros((6, n_ch), F32)], axis=0)
        ph, plo = _split2(pz)
        imp = _nn(ph, ov_ref[...]) + _nn(plo, ov_ref[...])
        jj = lax.broadcasted_iota(jnp.int32, (8, n_sbp), 1)
        cur = p_len // SEL_BLK
        forced = (jj == 0) | (jj == cur) | (jj == cur - 1)
        score = jnp.where(jj < n_sb, jnp.where(forced, jnp.inf, imp), -jnp.inf)
        lane = lax.broadcasted_iota(jnp.int32, (8, LANES), 1)
        idx = jnp.zeros((8, LANES), jnp.int32)
        for n in range(N_SEL):
            mx = jnp.max(score, axis=1, keepdims=True)
            first = jnp.min(jnp.where(score == mx, jj, n_sbp), axis=1, keepdims=True)
            idx = jnp.where(lane == n, first, idx)
            score = jnp.where(jj == first, -jnp.inf, score)
        idx_out[0] = idx


def _decode_a(pt, pool3, q3, cmp_args, ov, s, p_len):
    n_pages = p_len // PAGE_SIZE
    n_ch = p_len // CMP_STRIDE
    n_sb = p_len // SEL_BLK + 1
    n_sbp = ov.shape[1]
    steps = n_pages // PAGES_PER_STEP

    def page_spec(i):
        return pl.BlockSpec((1, PAGE_SIZE, 256), lambda si, j, pt_ref: (pt_ref[si, j * PAGES_PER_STEP + i], 0, 0))

    full = lambda shape: pl.BlockSpec(shape, lambda si, j, pt_ref: (0,) * len(shape))
    cmp_specs = [full((CMP_STRIDE, 256)), full((CMP_STRIDE, 256)), full((CMP_STRIDE, 256, 256)),
                 full((CMP_STRIDE, 256, 256)), full((1, 256)), full((256, 256)), full((1, 256)),
                 full((1, LANES)), full((n_ch, LANES)), full((n_ch, LANES)), full((LANES, LANES))]
    grid_spec = pltpu.PrefetchScalarGridSpec(
        num_scalar_prefetch=1,
        grid=(s, steps),
        in_specs=[page_spec(i) for i in range(PAGES_PER_STEP)]
        + [pl.BlockSpec((1, 1, 1024), lambda si, j, pt_ref: (si, 0, 0))] + cmp_specs + [full(ov.shape)],
        out_specs=[pl.BlockSpec((1, 16, 256), lambda si, j, pt_ref: (si, 0, 0)),
                   pl.BlockSpec((1, 8, LANES), lambda si, j, pt_ref: (si, 0, 0))],
        scratch_shapes=[pltpu.VMEM((p_len, LANES), F32), pltpu.VMEM((p_len, LANES), F32)],
    )
    return pl.pallas_call(
        functools.partial(_dec_a_body, n_ch=n_ch, n_sb=n_sb, n_sbp=n_sbp, p_len=p_len),
        grid_spec=grid_spec,
        out_shape=[jax.ShapeDtypeStruct((s, 16, 256), F32), jax.ShapeDtypeStruct((s, 8, LANES), jnp.int32)],
        compiler_params=_cp(("arbitrary", "arbitrary")),
        name="decode_select",
    )(pt, *([pool3] * PAGES_PER_STEP), q3, *cmp_args, ov)


def _dec_b_body(pt_ref, idx_ref, *refs, n_blk, p_len):
    nsel = NSA_KV_HEADS * N_SEL
    blocks = refs[:nsel]
    q_ref, g_ref, oc_ref, new4_ref, neww_ref, win_ref, o_out, win_out, kall, vall = refs[nsel:]
    si = pl.program_id(0)
    qrow = q_ref[0]
    gates = g_ref[0]
    oc = oc_ref[0]
    new4 = new4_ref[0]
    neww = neww_ref[0]
    rows = lax.broadcasted_iota(jnp.int32, (16, LANES), 0)

    old = win_ref[0]
    wlen = old.shape[0]
    shifted = pltpu.roll(old, wlen - 1, axis=0)
    wrow = lax.broadcasted_iota(jnp.int32, (wlen, 1), 0)
    win_new = jnp.where(wrow == wlen - 1, neww, shifted)
    win_out[0] = win_new
    kwb = win_new[:, 0:128].astype(BF16)
    vwb = win_new[:, 128:256].astype(BF16)

    ksel_new = new4[:, 256:384].astype(BF16).astype(F32)
    vsel_new = new4[:, 384:512].astype(BF16).astype(F32)
    lane_k = lax.broadcasted_iota(jnp.int32, (16, N_SEL * SEL_BLK), 1)

    heads = []
    for g in range(NSA_KV_HEADS):
        qg = jnp.zeros((16, LANES), F32)
        for h in range(4):
            piece = qrow[:, (4 * g + h) * LANES:(4 * g + h + 1) * LANES].astype(F32)
            qg = jnp.where(rows == h, piece, qg)
        qgb = qg.astype(BF16)

        valid = lane_k < 0
        for n in range(N_SEL):
            blk = blocks[g * N_SEL + n][0]
            kall[n * SEL_BLK:(n + 1) * SEL_BLK, :] = blk[:, 0:128].astype(BF16)
            vall[n * SEL_BLK:(n + 1) * SEL_BLK, :] = blk[:, 128:256].astype(BF16)
            in_pool = idx_ref[si, g * N_SEL + n] < n_blk
            valid = valid | (in_pool & (lax.shift_right_logical(lane_k, 6) == n))
        s = _nt(qgb, kall[...])
        s_new = jnp.sum(qgb.astype(F32) * ksel_new, axis=1, keepdims=True)
        sm = jnp.where(valid, s, NEG)
        m = jnp.maximum(jnp.max(sm, axis=1, keepdims=True), s_new)
        p = jnp.where(valid, jnp.exp(sm - m), 0.0)
        p_new = jnp.exp(s_new - m)
        l = jnp.sum(p, axis=1, keepdims=True) + p_new
        o_s = (_nn(p.astype(BF16), vall[...]) + p_new.astype(BF16).astype(F32) * vsel_new) * (1.0 / l)

        sw = _nt(qgb, kwb)
        mw = jnp.max(sw, axis=1, keepdims=True)
        pw = jnp.exp(sw - mw)
        o_w = _nn(pw.astype(BF16), vwb) * (1.0 / jnp.sum(pw, axis=1, keepdims=True))

        o_c = oc[:, g * LANES:(g + 1) * LANES]
        for h in range(4):
            hd = 4 * g + h
            mix = (gates[:, 3 * hd:3 * hd + 1] * o_c[h:h + 1, :] + gates[:, 3 * hd + 1:3 * hd + 2] * o_s[h:h + 1, :]
                   + gates[:, 3 * hd + 2:3 * hd + 3] * o_w[h:h + 1, :])
            if hd % 2 != g:
                mix = pltpu.roll(mix, 64, axis=1)
            heads.append(mix)
    lane1 = lax.broadcasted_iota(jnp.int32, (1, LANES), 1)
    cols = [jnp.where(lane1 < 64, heads[2 * c], heads[2 * c + 1]) for c in range(4)]
    o_out[0] = jnp.concatenate(cols, axis=1)


def _decode_b(pt, idx32, pool2, q3, g3, oc, new4, neww, cache_win3, s, p_len):
    n_blk = p_len // SEL_BLK
    nsel = NSA_KV_HEADS * N_SEL
    wlen = cache_win3.shape[1]

    def blk_spec(n):
        def imap(si, pt_ref, idx_ref):
            jb = jnp.minimum(idx_ref[si, n], n_blk - 1)
            return (pt_ref[si, jb // 2] * 2 + jb % 2, 0, 1)
        return pl.BlockSpec((1, SEL_BLK, 256), imap)

    per = lambda shape: pl.BlockSpec(shape, lambda si, pt_ref, idx_ref: (si,) + (0,) * (len(shape) - 1))
    grid_spec = pltpu.PrefetchScalarGridSpec(
        num_scalar_prefetch=2,
        grid=(s,),
        in_specs=[blk_spec(n) for n in range(nsel)]
        + [per((1, 1, 1024)), per((1, 1, LANES)), per((1, 16, 256)), per((1, 1, 512)), per((1, 1, 256)),
           per((1, wlen, 256))],
        out_specs=[per((1, 1, 512)), per((1, wlen, 256))],
        scratch_shapes=[pltpu.VMEM((N_SEL * SEL_BLK, LANES), BF16), pltpu.VMEM((N_SEL * SEL_BLK, LANES), BF16)],
    )
    return pl.pallas_call(
        functools.partial(_dec_b_body, n_blk=n_blk, p_len=p_len),
        grid_spec=grid_spec,
        out_shape=[jax.ShapeDtypeStruct((s, 1, 512), F32), jax.ShapeDtypeStruct((s, wlen, 256), F32)],
        compiler_params=_cp(("arbitrary",)),
        name="decode_attend",
    )(pt, idx32, *([pool2] * nsel), q3, g3, oc, new4, neww, cache_win3)


def _mem_dec_body(qm_ref, kv_ref, o_ref):
    qrow = qm_ref[0]
    kv = kv_ref[0]
    kb = kv[:, 0:256].astype(BF16)
    vb = kv[:, 256:512].astype(BF16)
    rows = lax.broadcasted_iota(jnp.int32, (16, 256), 0)
    lanes = lax.broadcasted_iota(jnp.int32, (16, 256), 1)
    qr = jnp.zeros((16, 256), F32)
    for h in range(MEM_HEADS):
        piece = qrow[:, h * LANES:(h + 1) * LANES].astype(F32)
        col = h // 2
        wide = jnp.concatenate([piece if c == col else jnp.zeros_like(piece) for c in range(2)], axis=1)
        qr = jnp.where(rows == h, wide, qr)
    s = _nt(qr.astype(BF16), kb)
    m = jnp.max(s, axis=1, keepdims=True)
    p = jnp.exp(s - m)
    p = p * (1.0 / jnp.sum(p, axis=1, keepdims=True))
    of = _nn(p.astype(BF16), vb)
    keep = rows == lax.shift_right_logical(lanes, 6)
    o_ref[0] = jnp.sum(jnp.where(keep, of, 0.0), axis=0, keepdims=True)


def _mem_decode(qm3, cache_mem3):
    s, m, _ = cache_mem3.shape
    per = lambda shape: pl.BlockSpec(shape, lambda i: (i,) + (0,) * (len(shape) - 1))
    return pl.pallas_call(
        _mem_dec_body,
        grid=(s,),
        in_specs=[per((1, 1, 512)), per((1, m, 512))],
        out_specs=per((1, 1, MEM_W)),
        out_shape=jax.ShapeDtypeStruct((s, 1, MEM_W), F32),
        compiler_params=_cp(("arbitrary",)),
        name="mem_decode",
    )(qm3, cache_mem3)


def _prep_w_in(w_in):
    d = w_in.shape[0]
    q_w, kv_w, g_w, zn_w, u_w, zr_w, qm_w, zm_w = jnp.split(
        w_in, [int(c) for c in np.cumsum([512, 768, 24, 512, 896, 256, 256])], axis=1)
    zero64 = jnp.zeros((d, HEAD_DIM), w_in.dtype)
    q_slots = []
    for hd in range(NSA_HEADS):
        wq = q_w[:, hd * HEAD_DIM:(hd + 1) * HEAD_DIM]
        q_slots += [wq, zero64] if hd // NSA_HPG == 0 else [zero64, wq]
    qm_slots = []
    for h in range(MEM_HEADS):
        wq = qm_w[:, h * HEAD_DIM:(h + 1) * HEAD_DIM]
        qm_slots += [wq, zero64] if h % 2 == 0 else [zero64, wq]
    g_pad = jnp.pad(g_w, ((0, 0), (0, LANES - g_w.shape[1])))
    w_p = jnp.concatenate(q_slots + [kv_w, zn_w, zr_w, zm_w, u_w] + qm_slots + [g_pad], axis=1)
    return w_p.astype(BF16)


def _rope_table(pos):
    half = HEAD_DIM // 2
    inv = ROPE_THETA ** (-2.0 * jnp.arange(half, dtype=F32) / HEAD_DIM)
    ang = pos.astype(F32)[:, None] * inv[None, :]
    cos = jnp.cos(ang)
    sin = jnp.sin(ang)
    return jnp.tile(jnp.concatenate([cos, cos], axis=1), (1, 2)), jnp.tile(jnp.concatenate([-sin, sin], axis=1), (1, 2))


def _block_diag(blocks):
    n = len(blocks)
    r, c = blocks[0].shape
    out = jnp.zeros((n * r, n * c), blocks[0].dtype)
    for i, blk in enumerate(blocks):
        out = out.at[i * r:(i + 1) * r, i * c:(i + 1) * c].set(blk)
    return out


def _prep_compress(cmp_pe, cmp_w1, cmp_b1, cmp_w2, cmp_b2, ck_norm, cend):
    eg = [(0, 0), (0, 1), (1, 0), (1, 1)]
    pelo = jnp.concatenate([cmp_pe[e, :CMP_STRIDE] for e, _ in eg], axis=1)
    pehi = jnp.concatenate([cmp_pe[e, CMP_STRIDE:] for e, _ in eg], axis=1)
    w1lo = jnp.stack([_block_diag([cmp_w1[e, p] for e, _ in eg]) for p in range(CMP_STRIDE)]).astype(BF16)
    w1hi = jnp.stack([_block_diag([cmp_w1[e, CMP_STRIDE + p] for e, _ in eg]) for p in range(CMP_STRIDE)]).astype(BF16)
    b1 = jnp.concatenate([cmp_b1[e] for e, _ in eg])[None, :]
    w2 = _block_diag([cmp_w2[e] for e, _ in eg]).astype(BF16)
    b2 = jnp.concatenate([cmp_b2[e] for e, _ in eg])[None, :]
    ckn = jnp.tile(ck_norm, 2)[None, :]
    cos_c, sin_c = _rope_table(cend)
    return [pelo, pehi, w1lo, w1hi, b1, w2, b2, ckn, cos_c, sin_c]


def _overlap(n_cb, n_sb):
    cstart = np.arange(n_cb)[:, None] * CMP_STRIDE
    sstart = np.arange(n_sb)[None, :] * SEL_BLK
    ov = np.clip(np.minimum(cstart + CMP_BLK, sstart + SEL_BLK) - np.maximum(cstart, sstart), 0, None)
    return ov.astype(np.float32) / CMP_BLK


def _pad_to(a, rows, cols):
    return np.pad(a, ((0, rows - a.shape[0]), (0, cols - a.shape[1])))


def kernel(x_prompt, x_sample, mem_prompt, cache_nsa, cache_win, cache_mem, state_rwkv_shift, state_rwkv_wkv,
           page_table, ln_g, w_in, nsa_q_norm, nsa_k_norm, cmp_pe, cmp_w1, cmp_b1, cmp_w2, cmp_b2, rwkv_mu, rwkv_w0,
           rwkv_w2, rwkv_a0, rwkv_a2, rwkv_k_k, rwkv_k_a, rwkv_r_k, rwkv_ln_w, rwkv_ln_b, mem_norm_g, w_mem_kv,
           mem_q_norm, mem_k_norm, w_out):
    depth = ln_g.shape[0]
    assert depth == 1
    bp, tp, d = x_prompt.shape
    bs, ts, _ = x_sample.shape
    assert ts == 1
    n_pages = page_table.shape[1]
    p_len = n_pages * PAGE_SIZE
    n_pool = cache_nsa.shape[1]
    wlen = cache_win.shape[2]
    mlen = cache_mem.shape[2]
    assert tp % KTILE == 0 and tp >= WINDOW + QBLK and n_pages % PAGES_PER_STEP == 0 and wlen == WINDOW
    l = 0

    bd = jnp.asarray(np.kron(np.eye(2), np.ones((HEAD_DIM, HEAD_DIM))), BF16)
    w_p = _prep_w_in(w_in[l])
    lng = ln_g[l][None, :]
    qn = jnp.tile(nsa_q_norm[l], 16)[None, :]
    kn = jnp.tile(nsa_k_norm[l], (1, 2))
    mqn = jnp.tile(mem_q_norm[l], 8)[None, :]
    w_out_b = w_out[l].astype(BF16)
    rw_args = [rwkv_mu[l][None, :], rwkv_w0[l][None, :], rwkv_w2[l], rwkv_a0[l][None, :], rwkv_a2[l],
               rwkv_k_k[l][None, :], rwkv_k_a[l][None, :], rwkv_r_k[l].reshape(1, RWKV_W),
               rwkv_ln_w[l][None, :], rwkv_ln_b[l][None, :]]

    tb = 256
    xp2 = x_prompt.reshape(bp * tp, d)
    cos_p, sin_p = _rope_table(jnp.arange(tp))
    q, rows4, rows_w, kvb, zs, u_rw, qm, gates = _project(xp2, lng, w_p, cos_p, sin_p, tp // tb, qn, kn, mqn, bd, tb)

    n_ch = tp // CMP_STRIDE
    cend_p = jnp.arange(n_ch) * CMP_STRIDE + CMP_BLK - 1
    cmp_args_p = _prep_compress(cmp_pe[l], cmp_w1[l], cmp_b1[l], cmp_w2[l], cmp_b2[l], nsa_k_norm[l, 0], cend_p) + [bd]
    ck, cv = _compress_prompt(rows4, cmp_args_p, bp, tp)
    n_sb = tp // SEL_BLK
    n_sbp = -(-n_sb // LANES) * LANES
    ov = _overlap(n_ch - 1, n_sb)
    ovt = jnp.asarray(_pad_to(ov.T, n_sbp, n_ch), BF16)
    key_blk = (np.arange(tp) // SEL_BLK)[:, None] == np.arange(n_sbp)[None, :]
    et = jnp.asarray(key_blk.reshape(tp // KTILE, KTILE, n_sbp), BF16)
    o_nsa = _nsa_prompt(q, gates, kvb, ck, cv, ovt, et, bp, tp)

    chunk = 64
    o_rw, wkv_p = _rwkv(u_rw, jnp.zeros((bp, RWKV_HEADS, HEAD_DIM, HEAD_DIM), F32),
                        jnp.zeros((bp, 1, RWKV_SHIFT_W), F32), rw_args, bd, bp, tp // chunk, chunk, chunk)

    kmn = jnp.tile(mem_k_norm[l], 4)[None, :]
    memkv_p = _mem_kv(mem_prompt, mem_norm_g[l][None, :], w_mem_kv[l].astype(BF16), kmn, bd)
    o_mem = _mem_attn(qm, memkv_p, bp, tp, 512)
    y_prompt = _out_proj(xp2, o_nsa, o_rw, o_mem, zs, w_out_b, 256).reshape(bp, tp, d)

    nsa_rows_prompt = rows4.reshape(1, bp, tp, 4, NSA_KV_HEADS, HEAD_DIM)
    win_p = min(WINDOW, tp)
    win_prompt = rows_w.reshape(bp, tp, 2, NSA_KV_HEADS, HEAD_DIM)[None, :, tp - win_p:]
    shift_prompt = u_rw.reshape(bp, tp, RWKV_SHIFT_W)[None, :, -1]
    mem_kv_prompt = memkv_p.reshape(1, bp, mlen, 2, MEM_HEADS, HEAD_DIM)

    xs2 = x_sample.reshape(bs, d)
    cos_s, sin_s = _rope_table(jnp.full((bs,), p_len))
    q_s, rows4_s, rows_w_s, _, zs_s, u_s, qm_s, gates_s = _project(xs2, lng, w_p, cos_s, sin_s, 1, qn, kn, mqn, bd, bs)

    n_ch_s = p_len // CMP_STRIDE
    cend_s = jnp.arange(n_ch_s) * CMP_STRIDE + CMP_BLK - 1
    cmp_args_s = _prep_compress(cmp_pe[l], cmp_w1[l], cmp_b1[l], cmp_w2[l], cmp_b2[l], nsa_k_norm[l, 0], cend_s) + [bd]
    n_sb_s = p_len // SEL_BLK + 1
    n_sbp_s = -(-n_sb_s // LANES) * LANES
    ov_s = jnp.asarray(_pad_to(_overlap(n_ch_s - 1, n_sb_s), n_ch_s, n_sbp_s), BF16)
    pool3 = cache_nsa[l].reshape(n_pool, PAGE_SIZE, 512)
    oc, idx = _decode_a(page_table, pool3, q_s.reshape(bs, 1, 1024), cmp_args_s, ov_s, bs, p_len)
    idx32 = idx[:, 0:NSA_KV_HEADS, 0:N_SEL].reshape(bs, NSA_KV_HEADS * N_SEL)
    pool2 = cache_nsa[l].reshape(n_pool * 2, SEL_BLK, 512)
    o_nsa_s, win_s = _decode_b(page_table, idx32, pool2, q_s.reshape(bs, 1, 1024), gates_s.reshape(bs, 1, LANES), oc,
                               rows4_s.reshape(bs, 1, 512), rows_w_s.reshape(bs, 1, 256),
                               cache_win[l].reshape(bs, wlen, 256), bs, p_len)

    cpad = 16
    u_pad = jnp.pad(u_s[:, None, :], ((0, 0), (0, cpad - 1), (0, 0))).reshape(bs * cpad, RWKV_SHIFT_W)
    o_rw_s, wkv_s = _rwkv(u_pad, state_rwkv_wkv[l], state_rwkv_shift[l][:, None, :], rw_args, bd, bs, 1, cpad, 1)
    o_rw_s = o_rw_s.reshape(bs, cpad, RWKV_W)[:, 0]

    o_mem_s = _mem_decode(qm_s.reshape(bs, 1, 512), cache_mem[l].reshape(bs, mlen, 512)).reshape(bs, MEM_W)
    y_sample = _out_proj(xs2, o_nsa_s.reshape(bs, 512), o_rw_s, o_mem_s, zs_s, w_out_b, bs).reshape(bs, 1, d)

    nsa_rows_sample = rows4_s.reshape(1, bs, 1, 4, NSA_KV_HEADS, HEAD_DIM)
    win_sample = win_s.reshape(1, bs, wlen, 2, NSA_KV_HEADS, HEAD_DIM)
    shift_sample = u_s[None]
    return (y_prompt, y_sample, nsa_rows_prompt, win_prompt, shift_prompt, wkv_p[None], mem_kv_prompt,
            nsa_rows_sample, win_sample, shift_sample, wkv_s[None])
```

```python
import functools

import numpy as np
import jax
import jax.numpy as jnp
from jax import lax
from jax.experimental import pallas as pl
from jax.experimental.pallas import tpu as pltpu

F32 = jnp.float32
BF16 = jnp.bfloat16

HEAD_DIM = 64
NSA_HEADS = 8
NSA_KV_HEADS = 2
NSA_HPG = 4
RWKV_HEADS = 4
MEM_HEADS = 4
NSA_W = 512
RWKV_W = 256
MEM_W = 256
CMP_BLK = 32
CMP_STRIDE = 16
SEL_BLK = 64
N_SEL = 16
WINDOW = 512
PAGE_SIZE = 128
RWKV_SHIFT_W = 896
ROPE_THETA = 10000.0
RMS_EPS = 1e-6
GN_EPS = 64e-5
SCALE = HEAD_DIM ** -0.5
LOG2E = 1.4426950408889634

LANES = 128
QBLK = 128
KTILE = 512
RWKV_CHUNK = 64
NEG = -1e30

C_Q = 0
C_KV = 1024
C_Z = 1792
C_U = 2816
C_QM = 3712
C_G = 4224
N_INP = 4352

VMEM_LIMIT = 56 * 1024 * 1024


def _cp(sem):
    return pltpu.CompilerParams(dimension_semantics=sem, vmem_limit_bytes=VMEM_LIMIT)


def _nt(a, b):
    return lax.dot_general(a, b, (((1,), (1,)), ((), ())), preferred_element_type=F32)


def _nn(a, b):
    return jnp.dot(a, b, preferred_element_type=F32)


def _nnb(a, b):
    return _nn(a.astype(BF16), b.astype(BF16))


def _ntb(a, b):
    return _nt(a.astype(BF16), b.astype(BF16))


def _split2(x):
    hi = x.astype(BF16)
    lo = (x - hi.astype(F32)).astype(BF16)
    return hi, lo


def _split3(x):
    hi = x.astype(BF16)
    r = x - hi.astype(F32)
    mid = r.astype(BF16)
    lo = (r - mid.astype(F32)).astype(BF16)
    return hi, mid, lo


def _nn_x(a, b):
    ah, al = _split2(a)
    bh, bl = _split2(b)
    return _nn(ah, bh) + _nn(ah, bl) + _nn(al, bh)


def _segsum(x, bd):
    cols = []
    for c in range(x.shape[1] // LANES):
        hi, lo = _split2(x[:, c * LANES:(c + 1) * LANES])
        cols.append(_nn(hi, bd) + _nn(lo, bd))
    return cols[0] if len(cols) == 1 else jnp.concatenate(cols, axis=1)


def _rot_half(x):
    lane = lax.broadcasted_iota(jnp.int32, x.shape, 1)
    up = pltpu.roll(x, 96, axis=1)
    dn = pltpu.roll(x, 32, axis=1)
    return jnp.where((lane & 63) < 32, up, dn)


def _rope_cols(v, cosf, sinf):
    cols = []
    for c in range(v.shape[1] // LANES):
        xc = v[:, c * LANES:(c + 1) * LANES]
        cols.append(xc * cosf + _rot_half(xc) * sinf)
    return cols[0] if len(cols) == 1 else jnp.concatenate(cols, axis=1)


def _head_rms(v, gain, bd):
    ss = _segsum(v * v, bd)
    return v * lax.rsqrt(ss * (1.0 / HEAD_DIM) + RMS_EPS) * gain


def _sigmoid(x):
    return 1.0 / (1.0 + jnp.exp(-x))


def _eye(n, dtype=BF16):
    r0 = lax.broadcasted_iota(jnp.int32, (n, n), 0)
    r1 = lax.broadcasted_iota(jnp.int32, (n, n), 1)
    return jnp.where(r0 == r1, 1.0, 0.0).astype(dtype)


def _softmax2_cols(s, bias, col_ok=None):
    ps = []
    for h in range(s.shape[1] // LANES):
        sm = s[:, h * LANES:(h + 1) * LANES] + bias
        m = jnp.max(sm, axis=0, keepdims=True)
        p = jnp.exp2(sm - m)
        r = 1.0 / jnp.sum(p, axis=0, keepdims=True)
        if col_ok is not None:
            r = jnp.where(col_ok, r, 0.0)
        ps.append(p * r)
    return jnp.concatenate(ps, axis=1)


def _full(shape):
    nd = len(shape)
    return pl.BlockSpec(shape, lambda *_: (0,) * nd)


def _proj_body(x_ref, lng_ref, w_ref, cos_ref, sin_ref, qn_ref, kn_ref, mqn_ref, bd_ref,
               q_out, rows4_out, rows4t_out, rowsw_out, kvb_out, zs_out, u_out, qm_out, g_out):
    x = x_ref[...]
    ms = jnp.mean(x * x, axis=-1, keepdims=True)
    xn = (x * lax.rsqrt(ms + RMS_EPS) * lng_ref[...]).astype(BF16)
    bd = bd_ref[...]
    cosf = cos_ref[...]
    sinf = sin_ref[...]

    def seg(a, b):
        return jnp.dot(xn, w_ref[:, a:b], preferred_element_type=F32)

    q = _rope_cols(_head_rms(seg(C_Q, C_KV), qn_ref[...], bd), cosf, sinf) * (SCALE * LOG2E)
    q_out[...] = q.astype(BF16)

    kv = seg(C_KV, C_Z)
    kn = kn_ref[...]
    ksel = _rope_cols(_head_rms(kv[:, 256:384], kn[1:2], bd), cosf, sinf)
    kwin = _rope_cols(_head_rms(kv[:, 512:640], kn[2:3], bd), cosf, sinf)
    rows4 = jnp.concatenate([kv[:, 0:256], ksel, kv[:, 384:512]], axis=1)
    rows4_out[...] = rows4
    rows4t_out[0] = rows4.T
    rowsw_out[:, 0:128] = kwin
    rowsw_out[:, 128:256] = kv[:, 640:768]
    kvb_out[:, 0:128] = ksel.astype(BF16)
    kvb_out[:, 128:256] = kv[:, 384:512].astype(BF16)
    kvb_out[:, 256:384] = kwin.astype(BF16)
    kvb_out[:, 384:512] = kv[:, 640:768].astype(BF16)

    z = seg(C_Z, C_U)
    zs_out[...] = z * _sigmoid(z)
    u_out[...] = seg(C_U, C_QM)
    qm = _head_rms(seg(C_QM, C_G), mqn_ref[...], bd) * (SCALE * LOG2E)
    qm_out[...] = qm.astype(BF16)
    g_out[...] = _sigmoid(seg(C_G, N_INP))


def _project(x2d, nb, t, lng, w_p, cos_t, sin_t, n_tab_blocks, qn, kn, mqn, bd, tb):
    n, d = x2d.shape
    nblk = t // tb
    row = lambda i: (i, 0)
    tab = lambda i: (i % n_tab_blocks, 0)
    outs = [
        jax.ShapeDtypeStruct((n, 1024), BF16),
        jax.ShapeDtypeStruct((n, 512), F32),
        jax.ShapeDtypeStruct((nb, 512, t), F32),
        jax.ShapeDtypeStruct((n, 256), F32),
        jax.ShapeDtypeStruct((n, 512), BF16),
        jax.ShapeDtypeStruct((n, 1024), F32),
        jax.ShapeDtypeStruct((n, RWKV_SHIFT_W), F32),
        jax.ShapeDtypeStruct((n, 512), BF16),
        jax.ShapeDtypeStruct((n, 128), F32),
    ]
    out_specs = [pl.BlockSpec((tb, o.shape[1]), row) for o in outs]
    out_specs[2] = pl.BlockSpec((1, 512, tb), lambda i: (i // nblk, 0, i % nblk))
    return pl.pallas_call(
        _proj_body,
        grid=(n // tb,),
        in_specs=[
            pl.BlockSpec((tb, d), row),
            _full((1, d)),
            _full((d, N_INP)),
            pl.BlockSpec((tb, LANES), tab),
            pl.BlockSpec((tb, LANES), tab),
            _full((1, 1024)),
            _full((3, LANES)),
            _full((1, 512)),
            _full((LANES, LANES)),
        ],
        out_specs=out_specs,
        out_shape=outs,
        compiler_params=_cp(("arbitrary",)),
        name="proj",
    )(x2d, lng, w_p, cos_t, sin_t, qn, kn, mqn, bd)


def _compress_compute(xk_ref, xv_ref, n_ch, pelo_ref, pehi_ref, w1lo_ref, w1hi_ref, b1_ref, w2_ref, b2_ref,
                      ckn_ref, cos_ref, sin_ref, bd_ref):
    lo = jnp.zeros((n_ch, 256), F32)
    hi = jnp.zeros((n_ch, 256), F32)
    for p in range(CMP_STRIDE):
        xp = jnp.concatenate([xk_ref[pl.ds(p, n_ch, stride=CMP_STRIDE), :],
                              xv_ref[pl.ds(p, n_ch, stride=CMP_STRIDE), :]], axis=1)
        lo = lo + _nn((xp + pelo_ref[p:p + 1, :]).astype(BF16), w1lo_ref[p])
        hi = hi + _nn((xp + pehi_ref[p:p + 1, :]).astype(BF16), w1hi_ref[p])
    his = pltpu.roll(hi, n_ch - 1, axis=0)
    pre = lo + his + b1_ref[...]
    hid = pre * _sigmoid(pre)
    out2 = _nn(hid.astype(BF16), w2_ref[...]) + b2_ref[...]
    ck = _head_rms(out2[:, 0:128], ckn_ref[...], bd_ref[...])
    ck = _rope_cols(ck, cos_ref[...], sin_ref[...])
    return ck, out2[:, 128:256]


def _compress_body(xk_ref, xv_ref, pelo_ref, pehi_ref, w1lo_ref, w1hi_ref, b1_ref, w2_ref, b2_ref,
                   ckn_ref, cos_ref, sin_ref, bd_ref, ck_out, cv_out, *, n_ch):
    ck, cv = _compress_compute(xk_ref, xv_ref, n_ch, pelo_ref, pehi_ref, w1lo_ref, w1hi_ref, b1_ref, w2_ref,
                               b2_ref, ckn_ref, cos_ref, sin_ref, bd_ref)
    ck_out[0] = ck.astype(BF16)
    cv_out[0] = cv.astype(BF16)


def _cmp_specs(n_ch):
    return [
        _full((CMP_STRIDE, 256)), _full((CMP_STRIDE, 256)),
        _full((CMP_STRIDE, 256, 256)), _full((CMP_STRIDE, 256, 256)),
        _full((1, 256)), _full((256, 256)), _full((1, 256)),
        _full((1, LANES)), _full((n_ch, LANES)), _full((n_ch, LANES)), _full((LANES, LANES)),
    ]


def _compress_prompt(rows4, cmp_args, b, t):
    n_ch = t // CMP_STRIDE
    outs = [jax.ShapeDtypeStruct((b, n_ch, LANES), BF16)] * 2
    return pl.pallas_call(
        functools.partial(_compress_body, n_ch=n_ch),
        grid=(b,),
        in_specs=[pl.BlockSpec((t, LANES), lambda i: (i, 0)), pl.BlockSpec((t, LANES), lambda i: (i, 1))]
        + _cmp_specs(n_ch),
        out_specs=[pl.BlockSpec((1, n_ch, LANES), lambda i: (i, 0, 0))] * 2,
        out_shape=outs,
        compiler_params=_cp(("arbitrary",)),
        name="compress_prompt",
    )(rows4, rows4, *cmp_args)


def _select_blocks(score, n_sel):
    jj = lax.broadcasted_iota(jnp.int32, score.shape, 0)
    big = jnp.int32(score.shape[0])
    sel = jnp.zeros(score.shape, F32)
    for _ in range(n_sel):
        mx = jnp.max(score, axis=0, keepdims=True)
        first = jnp.min(jnp.where(score == mx, jj, big), axis=0, keepdims=True)
        pick = jj == first
        sel = jnp.where(pick, 1.0, sel)
        score = jnp.where(pick, -jnp.inf, score)
    return sel


def _nsa_prompt_body(q_ref, g_ref, kvb_ref, ck_ref, cv_ref, ovt_ref, et_ref, o_ref, *, n_cb, n_sb):
    i = pl.program_id(1)
    start = i * QBLK
    q = q_ref[...]
    eye = _eye(LANES)
    r1 = lax.broadcasted_iota(jnp.int32, (LANES, LANES), 1)
    gT = g_ref[...].T
    qpos = start + lax.broadcasted_iota(jnp.int32, (1, QBLK), 1)
    cvT = _nt(eye, cv_ref[0]).astype(BF16)
    ck = ck_ref[0]
    ovt = ovt_ref[...]
    wk = WINDOW + QBLK
    n_tiles = (start + QBLK + KTILE - 1) // KTILE
    lo_w = pl.multiple_of(jnp.maximum(start - WINDOW, 0), QBLK)

    cend = lax.broadcasted_iota(jnp.int32, (n_cb, 1), 0) * CMP_STRIDE + (CMP_BLK - 1)
    bias_c = jnp.where(cend <= qpos, 0.0, NEG)
    ok_c = qpos >= CMP_BLK - 1
    kpos_w = lo_w + lax.broadcasted_iota(jnp.int32, (wk, 1), 0)
    bias_w = jnp.where((kpos_w <= qpos) & (kpos_w > qpos - WINDOW), 0.0, NEG)

    heads = []
    for g in range(NSA_KV_HEADS):
        qg = jnp.concatenate([q[:, (4 * g + h) * LANES:(4 * g + h + 1) * LANES] for h in range(4)], axis=0)

        p = _softmax2_cols(_nt(ck, qg), bias_c, ok_c)
        ocT = _nn(cvT, p.astype(BF16))
        psum = p[:, 0:128] + p[:, 128:256] + p[:, 256:384] + p[:, 384:512]
        ph, plo = _split2(psum)
        imp = _nn(ovt, ph) + _nn(ovt, plo)

        jj = lax.broadcasted_iota(jnp.int32, (n_sb, QBLK), 0)
        cur = lax.shift_right_logical(qpos, 6)
        forced = (jj == 0) | (jj == cur) | (jj == cur - 1)
        score = jnp.where(jj * SEL_BLK <= qpos, jnp.where(forced, jnp.inf, imp), -jnp.inf)
        sel = _select_blocks(score, N_SEL).astype(BF16)

        def tile_step(t, carry):
            m, l, acc = carry
            k0 = pl.multiple_of(t * KTILE, KTILE)
            k = kvb_ref[pl.ds(k0, KTILE), 0:128]
            v = kvb_ref[pl.ds(k0, KTILE), 128:256]
            s = _nt(k, qg)
            mb = _nn(et_ref[t], sel)
            kpos = k0 + lax.broadcasted_iota(jnp.int32, (KTILE, 1), 0)
            bias = jnp.where((mb > 0.5) & (kpos <= qpos), 0.0, NEG)
            ms, ls, ps, als = [], [], [], []
            for h in range(4):
                cs = slice(h * LANES, (h + 1) * LANES)
                sm = s[:, cs] + bias
                mn = jnp.maximum(m[:, cs], jnp.max(sm, axis=0, keepdims=True))
                alpha = jnp.exp2(m[:, cs] - mn)
                pp = jnp.exp2(sm - mn)
                ls.append(alpha * l[:, cs] + jnp.sum(pp, axis=0, keepdims=True))
                ms.append(mn)
                als.append(alpha)
                ps.append(pp.astype(BF16))
            vT = _nt(eye, v).astype(BF16)
            acc = jnp.concatenate(als, axis=1) * acc + _nn(vT, jnp.concatenate(ps, axis=1))
            return jnp.concatenate(ms, axis=1), jnp.concatenate(ls, axis=1), acc

        m0 = jnp.full((1, 4 * QBLK), NEG, F32)
        l0 = jnp.zeros((1, 4 * QBLK), F32)
        a0 = jnp.zeros((LANES, 4 * QBLK), F32)
        _, l, acc = lax.fori_loop(0, n_tiles, tile_step, (m0, l0, a0))
        osT = acc * (1.0 / l)

        kw = kvb_ref[pl.ds(lo_w, wk), 256:384]
        vw = kvb_ref[pl.ds(lo_w, wk), 384:512]
        pw = _softmax2_cols(_nt(kw, qg), bias_w)
        owT = _nn(_nt(eye, vw).astype(BF16), pw.astype(BF16))

        for h in range(4):
            hd = 4 * g + h
            cs = slice(h * QBLK, (h + 1) * QBLK)
            mixT = (gT[3 * hd:3 * hd + 1, :] * ocT[:, cs] + gT[3 * hd + 1:3 * hd + 2, :] * osT[:, cs]
                    + gT[3 * hd + 2:3 * hd + 3, :] * owT[:, cs])
            mix = mixT.T
            if hd % 2 != g:
                mix = pltpu.roll(mix, 64, axis=1)
            heads.append(mix)

    cols = [jnp.where(r1 < 64, heads[2 * c], heads[2 * c + 1]) for c in range(4)]
    o_ref[...] = jnp.concatenate(cols, axis=1)


def _nsa_prompt(q, gates, kvb, ck, cv, ovt, et, b, t):
    nq = t // QBLK
    n_cb = ck.shape[1]
    n_sb = ovt.shape[0]
    return pl.pallas_call(
        functools.partial(_nsa_prompt_body, n_cb=n_cb, n_sb=n_sb),
        grid=(b, nq),
        in_specs=[
            pl.BlockSpec((QBLK, 1024), lambda bi, i: (bi * nq + i, 0)),
            pl.BlockSpec((QBLK, LANES), lambda bi, i: (bi * nq + i, 0)),
            pl.BlockSpec((t, 512), lambda bi, i: (bi, 0)),
            pl.BlockSpec((1, n_cb, LANES), lambda bi, i: (bi, 0, 0)),
            pl.BlockSpec((1, n_cb, LANES), lambda bi, i: (bi, 0, 0)),
            _full(ovt.shape),
            _full(et.shape),
        ],
        out_specs=pl.BlockSpec((QBLK, 512), lambda bi, i: (bi * nq + i, 0)),
        out_shape=jax.ShapeDtypeStruct((b * t, 512), F32),
        compiler_params=_cp(("arbitrary", "arbitrary")),
        name="nsa_prompt",
    )(q, gates, kvb, ck, cv, ovt, et)


def _rwkv_prep(u, mu, w0, w2, a0, a2, kkp, kap, bd):
    r = u[:, 0:256]
    k = u[:, 256:512]
    v = u[:, 512:768]
    ww = w0 + _nn_x(jnp.tanh(u[:, 768:832]), w2)
    nw = -ww
    sp = jnp.maximum(nw, 0.0) + jnp.log(1.0 + jnp.exp(-jnp.abs(nw)))
    logdec = -jnp.exp(-sp - 0.5)
    a = _sigmoid(a0 + _nn_x(u[:, 832:896], a2))
    kk = k * kkp
    kk = kk / jnp.maximum(jnp.sqrt(_segsum(kk * kk, bd)), 1e-12)
    k2 = k * (1.0 + (a - 1.0) * kap)
    return r, k2, v, kk, kk * a, logdec


def _rwkv_body(u_ref, mu_ref, w0_ref, w2_ref, a0_ref, a2_ref, kk_ref, ka_ref, rk_ref,
               lnw_ref, lnb_ref, bd_ref, o_ref, sout_ref, s_sc, prev_sc, *, nbat, n_dbl):
    c = pl.program_id(1)
    C = RWKV_CHUNK

    @pl.when(c == 0)
    def _():
        s_sc[...] = jnp.zeros(s_sc.shape, F32)
        prev_sc[...] = jnp.zeros(prev_sc.shape, F32)

    bd = bd_ref[...]
    rowi = lax.broadcasted_iota(jnp.int32, (C, 1), 0)
    ti = lax.broadcasted_iota(jnp.int32, (C, C), 0)
    ii = lax.broadcasted_iota(jnp.int32, (C, C), 1)
    tri = jnp.where(ti >= ii, 1.0, 0.0).astype(BF16)
    eye_c = jnp.where(ti == ii, 1.0, 0.0)
    strict = ti > ii
    incl = ti >= ii
    eye64 = _eye(HEAD_DIM)

    for n in range(nbat):
        ucur = u_ref[n]
        prev = jnp.where(rowi == 0, prev_sc[n], pltpu.roll(ucur, 1, axis=0))
        prev_sc[n] = ucur[C - 1:C, :]
        u = ucur + (prev - ucur) * mu_ref[...]
        r, k2, v, kk, bb, logdec = _rwkv_prep(u, mu_ref[...], w0_ref[...], w2_ref[...], a0_ref[...], a2_ref[...],
                                              kk_ref[...], ka_ref[...], bd)
        d1, d2, d3 = _split3(logdec)
        lcum = _nn(tri, d1) + _nn(tri, d2) + _nn(tri, d3)
        lend = lcum[C - 1:C, :]
        e_neg = jnp.exp(-lcum)
        at = -kk * jnp.exp(lcum - logdec)
        bt = bb * e_neg
        kt = k2 * e_neg
        rt = r * jnp.exp(lcum)
        e_end = jnp.exp(lend - lcum)
        bg = bb * e_end
        kg = k2 * e_end
        gend = jnp.exp(lend)

        ys = []
        for h in range(RWKV_HEADS):
            sl = slice(h * HEAD_DIM, (h + 1) * HEAD_DIM)
            s_old = s_sc[n, h]
            ar = jnp.concatenate([at[:, sl], rt[:, sl]], axis=0).astype(BF16)
            bk = jnp.concatenate([bt[:, sl], kt[:, sl]], axis=0).astype(BF16)
            vh = v[:, sl].astype(BF16)
            mm = _nt(ar, bk)
            lab = jnp.where(strict, mm[0:C, 0:C], 0.0)
            lak = jnp.where(strict, mm[0:C, C:2 * C], 0.0)
            mrb = jnp.where(incl, mm[C:2 * C, 0:C], 0.0)
            mrk = jnp.where(incl, mm[C:2 * C, C:2 * C], 0.0)
            ars = _nt(ar, s_old.astype(BF16))
            rhs = ars[0:C] + _nnb(lak, vh)
            x = eye_c + lab
            pw = lab
            for _ in range(n_dbl):
                pwb = pw.astype(BF16)
                pw = _nn(pwb, pwb)
                x = x + _nnb(pw, x)
            uu = _nnb(x, rhs)
            ys.append(ars[C:2 * C] + _nnb(mrb, uu) + _nnb(mrk, vh))
            uv = jnp.concatenate([uu.astype(BF16), vh], axis=0)
            uvt = _nt(eye64, uv).astype(BF16)
            bkg = jnp.concatenate([bg[:, sl], kg[:, sl]], axis=0).astype(BF16)
            s_sc[n, h] = s_old * gend[:, sl] + _nn(uvt, bkg)

        y = jnp.concatenate(ys, axis=1)
        mean = _segsum(y, bd) * (1.0 / HEAD_DIM)
        dy = y - mean
        var = _segsum(dy * dy, bd) * (1.0 / HEAD_DIM)
        yn = dy * lax.rsqrt(var + GN_EPS) * lnw_ref[...] + lnb_ref[...]
        bonus = _segsum(r * k2 * rk_ref[...], bd) * v
        o_ref[n] = yn + bonus

    @pl.when(c == pl.num_programs(1) - 1)
    def _():
        sout_ref[...] = s_sc[...]


def _rwkv_prompt(u3, rw_args, bd, nbat):
    b, t, _ = u3.shape
    C = RWKV_CHUNK
    n_dbl = int(np.log2(C)) - 1
    outs = [jax.ShapeDtypeStruct((b, t, RWKV_W), F32),
            jax.ShapeDtypeStruct((b, RWKV_HEADS, HEAD_DIM, HEAD_DIM), F32)]
    vec = _full((1, RWKV_W))
    return pl.pallas_call(
        functools.partial(_rwkv_body, nbat=nbat, n_dbl=n_dbl),
        grid=(b // nbat, t // C),
        in_specs=[
            pl.BlockSpec((nbat, C, RWKV_SHIFT_W), lambda bi, c: (bi, c, 0)),
            _full((1, RWKV_SHIFT_W)),
            vec, _full((HEAD_DIM, RWKV_W)), vec, _full((HEAD_DIM, RWKV_W)),
            vec, vec, vec, vec, vec, _full((LANES, LANES)),
        ],
        out_specs=[pl.BlockSpec((nbat, C, RWKV_W), lambda bi, c: (bi, c, 0)),
                   pl.BlockSpec((nbat, RWKV_HEADS, HEAD_DIM, HEAD_DIM), lambda bi, c: (bi, 0, 0, 0))],
        out_shape=outs,
        scratch_shapes=[pltpu.VMEM((nbat, RWKV_HEADS, HEAD_DIM, HEAD_DIM), F32),
                        pltpu.VMEM((nbat, 1, RWKV_SHIFT_W), F32)],
        compiler_params=_cp(("arbitrary", "arbitrary")),
        name="rwkv_prompt",
    )(u3, *rw_args, bd)


def _rwkv_dec_body(u_ref, sh_ref, st_ref, mu_ref, w0_ref, w2t_ref, a0_ref, a2t_ref, kk_ref, ka_ref, rk_ref,
                   lnw_ref, lnb_ref, o_ref, so_ref, a_sc, w_sc, b_sc, k_sc, r_sc, v_sc, y_sc):
    h = pl.program_id(0)
    nh = pl.num_programs(0)
    D = HEAD_DIM

    @pl.when(h == 0)
    def _():
        uT = u_ref[...].T
        pT = sh_ref[...].T
        xT = uT + (pT - uT) * mu_ref[...]
        rT = xT[0:256]
        kT = xT[256:512]
        ww = w0_ref[...] + _nn_x(w2t_ref[...], jnp.tanh(xT[768:832]))
        nw = -ww
        sp = jnp.maximum(nw, 0.0) + jnp.log(1.0 + jnp.exp(-jnp.abs(nw)))
        w_sc[...] = jnp.exp(-jnp.exp(-sp - 0.5))
        aT = _sigmoid(a0_ref[...] + _nn_x(a2t_ref[...], xT[832:896]))
        kk = kT * kk_ref[...]
        for hh in range(RWKV_HEADS):
            blk = kk[hh * D:(hh + 1) * D]
            nrm = jnp.maximum(jnp.sqrt(jnp.sum(blk * blk, axis=0, keepdims=True)), 1e-12)
            a_sc[hh * D:(hh + 1) * D, :] = -(blk / nrm)
        b_sc[...] = -a_sc[...] * aT
        k_sc[...] = kT * (1.0 + (aT - 1.0) * ka_ref[...])
        r_sc[...] = rT
        v_sc[...] = xT[512:768]

    base = pl.multiple_of(h * D, D)
    at = a_sc[pl.ds(base, D), :]
    wt = w_sc[pl.ds(base, D), :]
    bt = b_sc[pl.ds(base, D), :]
    kt = k_sc[pl.ds(base, D), :]
    rt = r_sc[pl.ds(base, D), :]

    def vstep(vi, carry):
        s_v = st_ref[0, vi]
        sa = jnp.sum(s_v * at, axis=0, keepdims=True)
        vv = v_sc[pl.ds(base + vi, 1), :]
        s_n = s_v * wt + sa * bt + vv * kt
        so_ref[0, vi] = s_n
        y_sc[pl.ds(base + vi, 1), :] = jnp.sum(s_n * rt, axis=0, keepdims=True)
        return carry

    lax.fori_loop(0, D, vstep, 0)

    @pl.when(h == nh - 1)
    def _():
        outs = []
        for hh in range(RWKV_HEADS):
            sl = slice(hh * D, (hh + 1) * D)
            y = y_sc[sl, :]
            mean = jnp.mean(y, axis=0, keepdims=True)
            dy = y - mean
            var = jnp.mean(dy * dy, axis=0, keepdims=True)
            yn = dy * lax.rsqrt(var + GN_EPS) * lnw_ref[sl, :] + lnb_ref[sl, :]
            bonus = jnp.sum(r_sc[sl, :] * k_sc[sl, :] * rk_ref[sl, :], axis=0, keepdims=True) * v_sc[sl, :]
            outs.append(yn + bonus)
        o_ref[...] = jnp.concatenate(outs, axis=0).T


def _rwkv_decode(u_s, shift, state_t, rw_cols):
    s = u_s.shape[0]
    D = HEAD_DIM
    col = lambda n: _full((n, 1))
    vec_sc = pltpu.VMEM((RWKV_W, s), F32)
    return pl.pallas_call(
        _rwkv_dec_body,
        grid=(RWKV_HEADS,),
        in_specs=[_full((s, RWKV_SHIFT_W)), _full((s, RWKV_SHIFT_W)),
                  pl.BlockSpec((1, D, D, s), lambda h: (h, 0, 0, 0)),
                  col(RWKV_SHIFT_W), col(RWKV_W), _full((RWKV_W, D)), col(RWKV_W), _full((RWKV_W, D)),
                  col(RWKV_W), col(RWKV_W), col(RWKV_W), col(RWKV_W), col(RWKV_W)],
        out_specs=[_full((s, RWKV_W)), pl.BlockSpec((1, D, D, s), lambda h: (h, 0, 0, 0))],
        out_shape=[jax.ShapeDtypeStruct((s, RWKV_W), F32), jax.ShapeDtypeStruct((RWKV_HEADS, D, D, s), F32)],
        scratch_shapes=[vec_sc] * 7,
        compiler_params=_cp(("arbitrary",)),
        name="rwkv_decode",
    )(u_s, shift, state_t, *rw_cols)


def _memkv_body(m_ref, g_ref, w_ref, kn_ref, bd_ref, o_ref):
    x = m_ref[0]
    ms = jnp.mean(x * x, axis=-1, keepdims=True)
    xn = (x * lax.rsqrt(ms + RMS_EPS) * g_ref[...]).astype(BF16)
    kv = _nn(xn, w_ref[...])
    o_ref[0, :, 0:256] = _head_rms(kv[:, 0:256], kn_ref[...], bd_ref[...])
    o_ref[0, :, 256:512] = kv[:, 256:512]


def _mem_kv(mem, g, w, kn, bd):
    b, m, d = mem.shape
    return pl.pallas_call(
        _memkv_body,
        grid=(b,),
        in_specs=[pl.BlockSpec((1, m, d), lambda i: (i, 0, 0)), _full((1, d)), _full((d, 512)),
                  _full((1, 256)), _full((LANES, LANES))],
        out_specs=pl.BlockSpec((1, m, 512), lambda i: (i, 0, 0)),
        out_shape=jax.ShapeDtypeStruct((b, m, 512), F32),
        compiler_params=_cp(("arbitrary",)),
        name="mem_kv",
    )(mem, g, w, kn, bd)


def _mem_attn_body(qm_ref, kv_ref, o_ref):
    qm = qm_ref[...]
    kvb = kv_ref[0].astype(BF16)
    eye = _eye(LANES)
    heads = []
    for h in range(MEM_HEADS):
        col = h // 2
        kc = kvb[:, col * LANES:(col + 1) * LANES]
        vc = kvb[:, 256 + col * LANES:256 + (col + 1) * LANES]
        s = _nt(kc, qm[:, h * LANES:(h + 1) * LANES])
        m = jnp.max(s, axis=0, keepdims=True)
        p = jnp.exp2(s - m)
        p = p * (1.0 / jnp.sum(p, axis=0, keepdims=True))
        oT = _nn(_nt(eye, vc).astype(BF16), p.astype(BF16))
        heads.append(oT.T)
    lane = lax.broadcasted_iota(jnp.int32, heads[0].shape, 1)
    o_ref[...] = jnp.concatenate([jnp.where(lane < 64, heads[0], heads[1]),
                                  jnp.where(lane < 64, heads[2], heads[3])], axis=1)


def _mem_attn(qm, memkv, nb, rows_per_b, tb):
    n = qm.shape[0]
    nblk = rows_per_b // tb
    m = memkv.shape[1]
    return pl.pallas_call(
        _mem_attn_body,
        grid=(nb, nblk),
        in_specs=[pl.BlockSpec((tb, 512), lambda b, i: (b * nblk + i, 0)),
                  pl.BlockSpec((1, m, 512), lambda b, i: (b, 0, 0))],
        out_specs=pl.BlockSpec((tb, MEM_W), lambda b, i: (b * nblk + i, 0)),
        out_shape=jax.ShapeDtypeStruct((n, MEM_W), F32),
        compiler_params=_cp(("arbitrary", "arbitrary")),
        name="mem_attn",
    )(qm, memkv)


def _out_body(x_ref, on_ref, or_ref, om_ref, zs_ref, w_ref, y_ref):
    zs = zs_ref[...]
    cat = jnp.concatenate([on_ref[...] * zs[:, 0:512], or_ref[...] * zs[:, 512:768],
                           om_ref[...] * zs[:, 768:1024]], axis=1)
    y_ref[...] = x_ref[...] + _nn(cat.astype(BF16), w_ref[...])


def _out_proj(x2d, o_nsa, o_rw, o_mem, zs, w_out, tb):
    n, d = x2d.shape
    row = lambda i: (i, 0)
    return pl.pallas_call(
        _out_body,
        grid=(n // tb,),
        in_specs=[pl.BlockSpec((tb, d), row), pl.BlockSpec((tb, 512), row), pl.BlockSpec((tb, 256), row),
                  pl.BlockSpec((tb, 256), row), pl.BlockSpec((tb, 1024), row), _full((1024, d))],
        out_specs=pl.BlockSpec((tb, d), row),
        out_shape=jax.ShapeDtypeStruct((n, d), F32),
        compiler_params=_cp(("arbitrary",)),
        name="out_proj",
    )(x2d, o_nsa, o_rw, o_mem, zs, w_out)


PAGES_PER_STEP = 8


def _query_rows(qrow, g):
    rows = lax.broadcasted_iota(jnp.int32, (16, LANES), 0)
    qg = jnp.zeros((16, LANES), F32)
    for h in range(4):
        piece = qrow[:, (4 * g + h) * LANES:(4 * g + h + 1) * LANES].astype(F32)
        qg = jnp.where(rows == h, piece, qg)
    return qg.astype(BF16)


def _dec_a_body(pt_ref, *refs, n_ch, n_sb, n_sbp, p_len):
    pages = refs[:PAGES_PER_STEP]
    (q_ref, pelo_ref, pehi_ref, w1lo_ref, w1hi_ref, b1_ref, w2_ref, b2_ref, ckn_ref, cos_ref, sin_ref,
     bd_ref, ov_ref, oc_out, idx_out, xk, xv) = refs[PAGES_PER_STEP:]
    j = pl.program_id(1)
    for i in range(PAGES_PER_STEP):
        r0 = pl.multiple_of((j * PAGES_PER_STEP + i) * PAGE_SIZE, PAGE_SIZE)
        xk[pl.ds(r0, PAGE_SIZE), :] = pages[i][0, 0].T
        xv[pl.ds(r0, PAGE_SIZE), :] = pages[i][0, 1].T

    @pl.when(j == pl.num_programs(1) - 1)
    def _():
        ck, cv = _compress_compute(xk, xv, n_ch, pelo_ref, pehi_ref, w1lo_ref, w1hi_ref, b1_ref, w2_ref, b2_ref,
                                   ckn_ref, cos_ref, sin_ref, bd_ref)
        ckb = ck.astype(BF16)
        cvb = cv.astype(BF16)
        qrow = q_ref[0]
        cend = lax.broadcasted_iota(jnp.int32, (1, n_ch), 1) * CMP_STRIDE + (CMP_BLK - 1)
        valid = cend <= p_len
        ocs = []
        imps = []
        for g in range(NSA_KV_HEADS):
            s = _nt(_query_rows(qrow, g), ckb)
            sm = jnp.where(valid, s, NEG)
            m = jnp.max(sm, axis=1, keepdims=True)
            p = jnp.where(valid, jnp.exp2(sm - m), 0.0)
            p = p * (1.0 / jnp.maximum(jnp.sum(p, axis=1, keepdims=True), 1e-30))
            ocs.append(_nn(p.astype(BF16), cvb))
            imps.append(p[0:1] + p[1:2] + p[2:3] + p[3:4])
        oc_out[0] = jnp.concatenate(ocs, axis=1)
        pz = jnp.concatenate(imps + [jnp.zeros((6, n_ch), F32)], axis=0)
        ph, plo = _split2(pz)
        imp = _nn(ph, ov_ref[...]) + _nn(plo, ov_ref[...])
        jj = lax.broadcasted_iota(jnp.int32, (8, n_sbp), 1)
        cur = p_len // SEL_BLK
        forced = (jj == 0) | (jj == cur) | (jj == cur - 1)
        score = jnp.where(jj < n_sb, jnp.where(forced, jnp.inf, imp), -jnp.inf)
        lane = lax.broadcasted_iota(jnp.int32, (8, LANES), 1)
        idx = jnp.zeros((8, LANES), jnp.int32)
        for n in range(N_SEL):
            mx = jnp.max(score, axis=1, keepdims=True)
            first = jnp.min(jnp.where(score == mx, jj, n_sbp), axis=1, keepdims=True)
            idx = jnp.where(lane == n, first, idx)
            score = jnp.where(jj == first, -jnp.inf, score)
        idx_out[0] = idx


def _decode_a(pt, pool_t, q3, cmp_args, ov, s, p_len):
    n_pages = p_len // PAGE_SIZE
    n_ch = p_len // CMP_STRIDE
    n_sb = p_len // SEL_BLK + 1
    n_sbp = ov.shape[1]
    steps = n_pages // PAGES_PER_STEP

    def page_spec(i):
        return pl.BlockSpec((1, 2, LANES, PAGE_SIZE),
                            lambda si, j, pt_ref: (pt_ref[si, j * PAGES_PER_STEP + i], 0, 0, 0))

    full = lambda shape: pl.BlockSpec(shape, lambda si, j, pt_ref: (0,) * len(shape))
    cmp_specs = [full((CMP_STRIDE, 256)), full((CMP_STRIDE, 256)), full((CMP_STRIDE, 256, 256)),
                 full((CMP_STRIDE, 256, 256)), full((1, 256)), full((256, 256)), full((1, 256)),
                 full((1, LANES)), full((n_ch, LANES)), full((n_ch, LANES)), full((LANES, LANES))]
    grid_spec = pltpu.PrefetchScalarGridSpec(
        num_scalar_prefetch=1,
        grid=(s, steps),
        in_specs=[page_spec(i) for i in range(PAGES_PER_STEP)]
        + [pl.BlockSpec((1, 1, 1024), lambda si, j, pt_ref: (si, 0, 0))] + cmp_specs + [full(ov.shape)],
        out_specs=[pl.BlockSpec((1, 16, 256), lambda si, j, pt_ref: (si, 0, 0)),
                   pl.BlockSpec((1, 8, LANES), lambda si, j, pt_ref: (si, 0, 0))],
        scratch_shapes=[pltpu.VMEM((p_len, LANES), F32), pltpu.VMEM((p_len, LANES), F32)],
    )
    return pl.pallas_call(
        functools.partial(_dec_a_body, n_ch=n_ch, n_sb=n_sb, n_sbp=n_sbp, p_len=p_len),
        grid_spec=grid_spec,
        out_shape=[jax.ShapeDtypeStruct((s, 16, 256), F32), jax.ShapeDtypeStruct((s, 8, LANES), jnp.int32)],
        compiler_params=_cp(("arbitrary", "arbitrary")),
        name="decode_select",
    )(pt, *([pool_t] * PAGES_PER_STEP), q3, *cmp_args, ov)


def _dec_b_body(pt_ref, idx_ref, pool_ref, q_ref, g_ref, oc_ref, new4_ref, neww_ref, win_ref, o_out, win_out,
                buf, sem, kall, vall, *, n_blk):
    nsel = NSA_KV_HEADS * N_SEL
    si = pl.program_id(0)
    n_seq = pl.num_programs(0)
    slot = lax.rem(si, 2)

    def page_copy(s, n, sl):
        jb = jnp.minimum(idx_ref[s, n], n_blk - 1)
        page = pt_ref[s, lax.shift_right_logical(jb, 1)]
        return pltpu.make_async_copy(pool_ref.at[page, pl.ds(2, 2)], buf.at[sl, n], sem.at[sl])

    @pl.when(si == 0)
    def _():
        for n in range(nsel):
            page_copy(0, n, 0).start()

    @pl.when(si + 1 < n_seq)
    def _():
        for n in range(nsel):
            page_copy(si + 1, n, 1 - slot).start()

    for n in range(nsel):
        page_copy(si, n, slot).wait()

    qrow = q_ref[0]
    gates = g_ref[0]
    oc = oc_ref[0]
    new4 = new4_ref[0]
    neww = neww_ref[0]

    wlen = win_ref.shape[3]
    r0 = lax.broadcasted_iota(jnp.int32, (LANES, LANES), 0)
    r1 = lax.broadcasted_iota(jnp.int32, (LANES, LANES), 1)
    lane_w = lax.broadcasted_iota(jnp.int32, (LANES, wlen), 1)
    new_t = []
    for c in range(2):
        col = jnp.sum(jnp.where(r0 == r1, neww[:, c * LANES:(c + 1) * LANES], 0.0), axis=1, keepdims=True)
        shifted = pltpu.roll(win_ref[0, c], wlen - 1, axis=1)
        nt = jnp.where(lane_w == wlen - 1, col, shifted)
        win_out[0, c] = nt
        new_t.append(nt.astype(BF16))
    kwT, vwT = new_t

    ksel_new = new4[:, 256:384].astype(BF16).astype(F32)
    vsel_new = new4[:, 384:512].astype(BF16).astype(F32)
    nk = N_SEL * PAGE_SIZE
    lane_k = lax.broadcasted_iota(jnp.int32, (16, nk), 1)

    heads = []
    for g in range(NSA_KV_HEADS):
        qgb = _query_rows(qrow, g)
        valid = lane_k < 0
        for n in range(N_SEL):
            kall[:, n * PAGE_SIZE:(n + 1) * PAGE_SIZE] = buf[slot, g * N_SEL + n, 0].astype(BF16)
            vall[:, n * PAGE_SIZE:(n + 1) * PAGE_SIZE] = buf[slot, g * N_SEL + n, 1].astype(BF16)
            jb = idx_ref[si, g * N_SEL + n]
            half = jb % 2
            valid = valid | ((jb < n_blk) & (lax.shift_right_logical(lane_k, 7) == n)
                             & ((lax.shift_right_logical(lane_k, 6) & 1) == half))
        s = _nn(qgb, kall[...])
        s_new = jnp.sum(qgb.astype(F32) * ksel_new, axis=1, keepdims=True)
        sm = jnp.where(valid, s, NEG)
        m = jnp.maximum(jnp.max(sm, axis=1, keepdims=True), s_new)
        p = jnp.where(valid, jnp.exp2(sm - m), 0.0)
        p_new = jnp.exp2(s_new - m)
        l = jnp.sum(p, axis=1, keepdims=True) + p_new
        o_s = (_nt(p.astype(BF16), vall[...]) + p_new.astype(BF16).astype(F32) * vsel_new) * (1.0 / l)

        sw = _nn(qgb, kwT)
        mw = jnp.max(sw, axis=1, keepdims=True)
        pw = jnp.exp2(sw - mw)
        o_w = _nt(pw.astype(BF16), vwT) * (1.0 / jnp.sum(pw, axis=1, keepdims=True))

        o_c = oc[:, g * LANES:(g + 1) * LANES]
        for h in range(4):
            hd = 4 * g + h
            mix = (gates[:, 3 * hd:3 * hd + 1] * o_c[h:h + 1, :] + gates[:, 3 * hd + 1:3 * hd + 2] * o_s[h:h + 1, :]
                   + gates[:, 3 * hd + 2:3 * hd + 3] * o_w[h:h + 1, :])
            if hd % 2 != g:
                mix = pltpu.roll(mix, 64, axis=1)
            heads.append(mix)
    lane1 = lax.broadcasted_iota(jnp.int32, (1, LANES), 1)
    cols = [jnp.where(lane1 < 64, heads[2 * c], heads[2 * c + 1]) for c in range(4)]
    o_out[0] = jnp.concatenate(cols, axis=1)


def _decode_b(pt, idx32, pool_t, q3, g3, oc, new4, neww, win_t, s, p_len):
    n_blk = p_len // SEL_BLK
    nsel = NSA_KV_HEADS * N_SEL
    wlen = win_t.shape[3]

    per = lambda shape: pl.BlockSpec(shape, lambda si, pt_ref, idx_ref: (si,) + (0,) * (len(shape) - 1))
    grid_spec = pltpu.PrefetchScalarGridSpec(
        num_scalar_prefetch=2,
        grid=(s,),
        in_specs=[pl.BlockSpec(memory_space=pl.ANY),
                  per((1, 1, 1024)), per((1, 1, LANES)), per((1, 16, 256)), per((1, 1, 512)), per((1, 1, 256)),
                  per((1, 2, LANES, wlen))],
        out_specs=[per((1, 1, 512)), per((1, 2, LANES, wlen))],
        scratch_shapes=[pltpu.VMEM((2, nsel, 2, LANES, PAGE_SIZE), F32), pltpu.SemaphoreType.DMA((2,)),
                        pltpu.VMEM((LANES, N_SEL * PAGE_SIZE), BF16), pltpu.VMEM((LANES, N_SEL * PAGE_SIZE), BF16)],
    )
    return pl.pallas_call(
        functools.partial(_dec_b_body, n_blk=n_blk),
        grid_spec=grid_spec,
        out_shape=[jax.ShapeDtypeStruct((s, 1, 512), F32), jax.ShapeDtypeStruct((s, 2, LANES, wlen), F32)],
        compiler_params=_cp(("arbitrary",)),
        name="decode_attend",
    )(pt, idx32, pool_t, q3, g3, oc, new4, neww, win_t)


def _mem_dec_body(qm_ref, kv_ref, o_ref):
    qrow = qm_ref[0]
    kT = kv_ref[0, 0].astype(BF16)
    vT = kv_ref[0, 1].astype(BF16)
    rows = lax.broadcasted_iota(jnp.int32, (16, 256), 0)
    lanes = lax.broadcasted_iota(jnp.int32, (16, 256), 1)
    qr = jnp.zeros((16, 256), F32)
    for h in range(MEM_HEADS):
        piece = qrow[:, h * LANES:(h + 1) * LANES].astype(F32)
        col = h // 2
        wide = jnp.concatenate([piece if c == col else jnp.zeros_like(piece) for c in range(2)], axis=1)
        qr = jnp.where(rows == h, wide, qr)
    s = _nn(qr.astype(BF16), kT)
    m = jnp.max(s, axis=1, keepdims=True)
    p = jnp.exp2(s - m)
    p = p * (1.0 / jnp.sum(p, axis=1, keepdims=True))
    of = _nt(p.astype(BF16), vT)
    keep = rows == lax.shift_right_logical(lanes, 6)
    o_ref[0] = jnp.sum(jnp.where(keep, of, 0.0), axis=0, keepdims=True)


def _mem_decode(qm3, mem_t):
    s, _, hd, m = mem_t.shape
    per = lambda shape: pl.BlockSpec(shape, lambda i: (i,) + (0,) * (len(shape) - 1))
    return pl.pallas_call(
        _mem_dec_body,
        grid=(s,),
        in_specs=[per((1, 1, 512)), per((1, 2, hd, m))],
        out_specs=per((1, 1, MEM_W)),
        out_shape=jax.ShapeDtypeStruct((s, 1, MEM_W), F32),
        compiler_params=_cp(("arbitrary",)),
        name="mem_decode",
    )(qm3, mem_t)


def _prep_w_in(w_in):
    d = w_in.shape[0]
    q_w, kv_w, g_w, zn_w, u_w, zr_w, qm_w, zm_w = jnp.split(
        w_in, [int(c) for c in np.cumsum([512, 768, 24, 512, 896, 256, 256])], axis=1)
    zero64 = jnp.zeros((d, HEAD_DIM), w_in.dtype)
    q_slots = []
    for hd in range(NSA_HEADS):
        wq = q_w[:, hd * HEAD_DIM:(hd + 1) * HEAD_DIM]
        q_slots += [wq, zero64] if hd // NSA_HPG == 0 else [zero64, wq]
    qm_slots = []
    for h in range(MEM_HEADS):
        wq = qm_w[:, h * HEAD_DIM:(h + 1) * HEAD_DIM]
        qm_slots += [wq, zero64] if h % 2 == 0 else [zero64, wq]
    g_pad = jnp.pad(g_w, ((0, 0), (0, LANES - g_w.shape[1])))
    w_p = jnp.concatenate(q_slots + [kv_w, zn_w, zr_w, zm_w, u_w] + qm_slots + [g_pad], axis=1)
    return w_p.astype(BF16)


def _rope_table(pos):
    half = HEAD_DIM // 2
    inv = ROPE_THETA ** (-2.0 * jnp.arange(half, dtype=F32) / HEAD_DIM)
    ang = pos.astype(F32)[:, None] * inv[None, :]
    cos = jnp.cos(ang)
    sin = jnp.sin(ang)
    return jnp.tile(jnp.concatenate([cos, cos], axis=1), (1, 2)), jnp.tile(jnp.concatenate([-sin, sin], axis=1), (1, 2))


def _block_diag(blocks):
    n = len(blocks)
    r, c = blocks[0].shape
    out = jnp.zeros((n * r, n * c), blocks[0].dtype)
    for i, blk in enumerate(blocks):
        out = out.at[i * r:(i + 1) * r, i * c:(i + 1) * c].set(blk)
    return out


def _prep_compress(cmp_pe, cmp_w1, cmp_b1, cmp_w2, cmp_b2, ck_norm, cend):
    eg = [(0, 0), (0, 1), (1, 0), (1, 1)]
    pelo = jnp.concatenate([cmp_pe[e, :CMP_STRIDE] for e, _ in eg], axis=1)
    pehi = jnp.concatenate([cmp_pe[e, CMP_STRIDE:] for e, _ in eg], axis=1)
    w1lo = jnp.stack([_block_diag([cmp_w1[e, p] for e, _ in eg]) for p in range(CMP_STRIDE)]).astype(BF16)
    w1hi = jnp.stack([_block_diag([cmp_w1[e, CMP_STRIDE + p] for e, _ in eg]) for p in range(CMP_STRIDE)]).astype(BF16)
    b1 = jnp.concatenate([cmp_b1[e] for e, _ in eg])[None, :]
    w2 = _block_diag([cmp_w2[e] for e, _ in eg]).astype(BF16)
    b2 = jnp.concatenate([cmp_b2[e] for e, _ in eg])[None, :]
    ckn = jnp.tile(ck_norm, 2)[None, :]
    cos_c, sin_c = _rope_table(cend)
    return [pelo, pehi, w1lo, w1hi, b1, w2, b2, ckn, cos_c, sin_c]


def _overlap(n_cb, n_sb):
    cstart = np.arange(n_cb)[:, None] * CMP_STRIDE
    sstart = np.arange(n_sb)[None, :] * SEL_BLK
    ov = np.clip(np.minimum(cstart + CMP_BLK, sstart + SEL_BLK) - np.maximum(cstart, sstart), 0, None)
    return ov.astype(np.float32) / CMP_BLK


def _pad_to(a, rows, cols):
    return np.pad(a, ((0, rows - a.shape[0]), (0, cols - a.shape[1])))


def kernel(x_prompt, x_sample, mem_prompt, cache_nsa, cache_win, cache_mem, state_rwkv_shift, state_rwkv_wkv,
           page_table, ln_g, w_in, nsa_q_norm, nsa_k_norm, cmp_pe, cmp_w1, cmp_b1, cmp_w2, cmp_b2, rwkv_mu, rwkv_w0,
           rwkv_w2, rwkv_a0, rwkv_a2, rwkv_k_k, rwkv_k_a, rwkv_r_k, rwkv_ln_w, rwkv_ln_b, mem_norm_g, w_mem_kv,
           mem_q_norm, mem_k_norm, w_out):
    depth = ln_g.shape[0]
    assert depth == 1
    bp, tp, d = x_prompt.shape
    bs, ts, _ = x_sample.shape
    assert ts == 1
    n_pages = page_table.shape[1]
    p_len = n_pages * PAGE_SIZE
    n_pool = cache_nsa.shape[1]
    wlen = cache_win.shape[2]
    mlen = cache_mem.shape[2]
    assert tp % KTILE == 0 and tp >= WINDOW + QBLK and n_pages % PAGES_PER_STEP == 0 and wlen == WINDOW
    l = 0

    bd = jnp.asarray(np.kron(np.eye(2), np.ones((HEAD_DIM, HEAD_DIM))), BF16)
    w_p = _prep_w_in(w_in[l])
    lng = ln_g[l][None, :]
    qn = jnp.tile(nsa_q_norm[l], 16)[None, :]
    kn = jnp.tile(nsa_k_norm[l], (1, 2))
    mqn = jnp.tile(mem_q_norm[l], 8)[None, :]
    w_out_b = w_out[l].astype(BF16)
    rw_vecs = [rwkv_mu[l], rwkv_w0[l], rwkv_a0[l], rwkv_k_k[l], rwkv_k_a[l], rwkv_r_k[l].reshape(RWKV_W),
               rwkv_ln_w[l], rwkv_ln_b[l]]
    mu, w0, a0, kkp, kap, rkp, lnw, lnb = rw_vecs
    rw_args = [mu[None, :], w0[None, :], rwkv_w2[l], a0[None, :], rwkv_a2[l], kkp[None, :], kap[None, :],
               rkp[None, :], lnw[None, :], lnb[None, :]]
    rw_cols = [mu[:, None], w0[:, None], rwkv_w2[l].T, a0[:, None], rwkv_a2[l].T, kkp[:, None], kap[:, None],
               rkp[:, None], lnw[:, None], lnb[:, None]]

    tb = 256
    xp2 = x_prompt.reshape(bp * tp, d)
    cos_p, sin_p = _rope_table(jnp.arange(tp))
    q, rows4, rows4t, rows_w, kvb, zs, u_rw, qm, gates = _project(
        xp2, bp, tp, lng, w_p, cos_p, sin_p, tp // tb, qn, kn, mqn, bd, tb)

    n_ch = tp // CMP_STRIDE
    cend_p = jnp.arange(n_ch) * CMP_STRIDE + CMP_BLK - 1
    cmp_args_p = _prep_compress(cmp_pe[l], cmp_w1[l], cmp_b1[l], cmp_w2[l], cmp_b2[l], nsa_k_norm[l, 0], cend_p) + [bd]
    ck, cv = _compress_prompt(rows4, cmp_args_p, bp, tp)
    n_sb = tp // SEL_BLK
    n_sbp = -(-n_sb // LANES) * LANES
    ov = _overlap(n_ch - 1, n_sb)
    ovt = jnp.asarray(_pad_to(ov.T, n_sbp, n_ch), BF16)
    key_blk = (np.arange(tp) // SEL_BLK)[:, None] == np.arange(n_sbp)[None, :]
    et = jnp.asarray(key_blk.reshape(tp // KTILE, KTILE, n_sbp), BF16)
    o_nsa = _nsa_prompt(q, gates, kvb, ck, cv, ovt, et, bp, tp)

    nbat = max(n for n in (2, 1) if bp % n == 0)
    o_rw, wkv_p = _rwkv_prompt(u_rw.reshape(bp, tp, RWKV_SHIFT_W), rw_args, bd, nbat)
    o_rw = o_rw.reshape(bp * tp, RWKV_W)

    kmn = jnp.tile(mem_k_norm[l], 4)[None, :]
    memkv_p = _mem_kv(mem_prompt, mem_norm_g[l][None, :], w_mem_kv[l].astype(BF16), kmn, bd)
    o_mem = _mem_attn(qm, memkv_p, bp, tp, 512)
    y_prompt = _out_proj(xp2, o_nsa, o_rw, o_mem, zs, w_out_b, 256).reshape(bp, tp, d)

    nsa_rows_prompt = rows4t.reshape(bp, 4, NSA_KV_HEADS, HEAD_DIM, tp).transpose(0, 4, 1, 2, 3)[None]
    win_p = min(WINDOW, tp)
    win_prompt = rows_w.reshape(bp, tp, 2, NSA_KV_HEADS, HEAD_DIM)[None, :, tp - win_p:]
    shift_prompt = u_rw.reshape(bp, tp, RWKV_SHIFT_W)[None, :, -1]
    mem_kv_prompt = memkv_p.reshape(1, bp, mlen, 2, MEM_HEADS, HEAD_DIM)

    xs2 = x_sample.reshape(bs, d)
    cos_s, sin_s = _rope_table(jnp.full((bs,), p_len))
    q_s, rows4_s, rows4t_s, rows_w_s, _, zs_s, u_s, qm_s, gates_s = _project(
        xs2, 1, bs, lng, w_p, cos_s, sin_s, 1, qn, kn, mqn, bd, bs)

    n_ch_s = p_len // CMP_STRIDE
    cend_s = jnp.arange(n_ch_s) * CMP_STRIDE + CMP_BLK - 1
    cmp_args_s = _prep_compress(cmp_pe[l], cmp_w1[l], cmp_b1[l], cmp_w2[l], cmp_b2[l], nsa_k_norm[l, 0], cend_s) + [bd]
    n_sb_s = p_len // SEL_BLK + 1
    n_sbp_s = -(-n_sb_s // LANES) * LANES
    ov_s = jnp.asarray(_pad_to(_overlap(n_ch_s - 1, n_sb_s), n_ch_s, n_sbp_s), BF16)
    pool_t = cache_nsa[l].transpose(0, 2, 3, 4, 1).reshape(n_pool, 4, NSA_KV_HEADS * HEAD_DIM, PAGE_SIZE)
    oc, idx = _decode_a(page_table, pool_t, q_s.reshape(bs, 1, 1024), cmp_args_s, ov_s, bs, p_len)
    idx32 = idx[:, 0:NSA_KV_HEADS, 0:N_SEL].reshape(bs, NSA_KV_HEADS * N_SEL)
    win_t = cache_win[l].transpose(0, 2, 3, 4, 1).reshape(bs, 2, NSA_KV_HEADS * HEAD_DIM, wlen)
    o_nsa_s, win_s = _decode_b(page_table, idx32, pool_t, q_s.reshape(bs, 1, 1024), gates_s.reshape(bs, 1, LANES), oc,
                               rows4_s.reshape(bs, 1, 512), rows_w_s.reshape(bs, 1, 256), win_t, bs, p_len)

    state_t = state_rwkv_wkv[l].transpose(1, 2, 3, 0)
    o_rw_s, wkv_s_t = _rwkv_decode(u_s, state_rwkv_shift[l], state_t, rw_cols)

    mem_t = cache_mem[l].transpose(0, 2, 3, 4, 1).reshape(bs, 2, MEM_HEADS * HEAD_DIM, mlen)
    o_mem_s = _mem_decode(qm_s.reshape(bs, 1, 512), mem_t).reshape(bs, MEM_W)
    y_sample = _out_proj(xs2, o_nsa_s.reshape(bs, 512), o_rw_s, o_mem_s, zs_s, w_out_b, bs).reshape(bs, 1, d)

    nsa_rows_sample = rows4t_s.reshape(4, NSA_KV_HEADS, HEAD_DIM, bs).transpose(3, 0, 1, 2)[None, :, None]
    win_sample = win_s.reshape(bs, 2, NSA_KV_HEADS, HEAD_DIM, wlen).transpose(0, 4, 1, 2, 3)[None]
    shift_sample = u_s[None]
    wkv_sample = wkv_s_t.transpose(3, 0, 1, 2)[None]
    return (y_prompt, y_sample, nsa_rows_prompt, win_prompt, shift_prompt, wkv_p[None], mem_kv_prompt,
            nsa_rows_sample, win_sample, shift_sample, wkv_sample)
```

```python
import functools

import numpy as np
import jax
import jax.numpy as jnp
from jax import lax
from jax.experimental import pallas as pl
from jax.experimental.pallas import tpu as pltpu

F32 = jnp.float32
BF16 = jnp.bfloat16

HEAD_DIM = 64
NSA_HEADS = 8
NSA_KV_HEADS = 2
NSA_HPG = 4
RWKV_HEADS = 4
MEM_HEADS = 4
NSA_W = 512
RWKV_W = 256
MEM_W = 256
CMP_BLK = 32
CMP_STRIDE = 16
SEL_BLK = 64
N_SEL = 16
WINDOW = 512
PAGE_SIZE = 128
RWKV_SHIFT_W = 896
ROPE_THETA = 10000.0
RMS_EPS = 1e-6
GN_EPS = 64e-5
SCALE = HEAD_DIM ** -0.5
LOG2E = 1.4426950408889634

LANES = 128
QBLK = 128
KTILE = 512
RWKV_CHUNK = 64
NEG = -1e30

C_Q = 0
C_KV = 1024
C_Z = 1792
C_U = 2816
C_QM = 3712
C_G = 4224
N_INP = 4352

VMEM_LIMIT = 56 * 1024 * 1024


def _cp(sem):
    return pltpu.CompilerParams(dimension_semantics=sem, vmem_limit_bytes=VMEM_LIMIT)


def _nt(a, b):
    return lax.dot_general(a, b, (((1,), (1,)), ((), ())), preferred_element_type=F32)


def _nn(a, b):
    return jnp.dot(a, b, preferred_element_type=F32)


def _nnb(a, b):
    return _nn(a.astype(BF16), b.astype(BF16))


def _ntb(a, b):
    return _nt(a.astype(BF16), b.astype(BF16))


def _split2(x):
    hi = x.astype(BF16)
    lo = (x - hi.astype(F32)).astype(BF16)
    return hi, lo


def _split3(x):
    hi = x.astype(BF16)
    r = x - hi.astype(F32)
    mid = r.astype(BF16)
    lo = (r - mid.astype(F32)).astype(BF16)
    return hi, mid, lo


def _nn_x(a, b):
    ah, al = _split2(a)
    bh, bl = _split2(b)
    return _nn(ah, bh) + _nn(ah, bl) + _nn(al, bh)


def _segsum(x, bd):
    cols = []
    for c in range(x.shape[1] // LANES):
        hi, lo = _split2(x[:, c * LANES:(c + 1) * LANES])
        cols.append(_nn(hi, bd) + _nn(lo, bd))
    return cols[0] if len(cols) == 1 else jnp.concatenate(cols, axis=1)


def _rot_half(x):
    lane = lax.broadcasted_iota(jnp.int32, x.shape, 1)
    up = pltpu.roll(x, 96, axis=1)
    dn = pltpu.roll(x, 32, axis=1)
    return jnp.where((lane & 63) < 32, up, dn)


def _rope_cols(v, cosf, sinf):
    cols = []
    for c in range(v.shape[1] // LANES):
        xc = v[:, c * LANES:(c + 1) * LANES]
        cols.append(xc * cosf + _rot_half(xc) * sinf)
    return cols[0] if len(cols) == 1 else jnp.concatenate(cols, axis=1)


def _head_rms(v, gain, bd):
    ss = _segsum(v * v, bd)
    return v * lax.rsqrt(ss * (1.0 / HEAD_DIM) + RMS_EPS) * gain


def _sigmoid(x):
    return 1.0 / (1.0 + jnp.exp(-x))


def _eye(n, dtype=BF16):
    r0 = lax.broadcasted_iota(jnp.int32, (n, n), 0)
    r1 = lax.broadcasted_iota(jnp.int32, (n, n), 1)
    return jnp.where(r0 == r1, 1.0, 0.0).astype(dtype)


def _softmax2_cols(s, bias, col_ok=None):
    ps = []
    for h in range(s.shape[1] // LANES):
        sm = s[:, h * LANES:(h + 1) * LANES] + bias
        m = jnp.max(sm, axis=0, keepdims=True)
        p = jnp.exp2(sm - m)
        r = 1.0 / jnp.sum(p, axis=0, keepdims=True)
        if col_ok is not None:
            r = jnp.where(col_ok, r, 0.0)
        ps.append(p * r)
    return jnp.concatenate(ps, axis=1)


def _full(shape):
    nd = len(shape)
    return pl.BlockSpec(shape, lambda *_: (0,) * nd)


def _proj_body(x_ref, lng_ref, w_ref, cos_ref, sin_ref, qn_ref, kn_ref, mqn_ref, bd_ref,
               q_out, rows4_out, rows4t_out, rowsw_out, kvb_out, vt_out, zs_out, u_out, qm_out, g_out):
    x = x_ref[...]
    ms = jnp.mean(x * x, axis=-1, keepdims=True)
    xn = (x * lax.rsqrt(ms + RMS_EPS) * lng_ref[...]).astype(BF16)
    bd = bd_ref[...]
    cosf = cos_ref[...]
    sinf = sin_ref[...]

    def seg(a, b):
        return jnp.dot(xn, w_ref[:, a:b], preferred_element_type=F32)

    q = _rope_cols(_head_rms(seg(C_Q, C_KV), qn_ref[...], bd), cosf, sinf) * (SCALE * LOG2E)
    q_out[...] = q.astype(BF16)

    kv = seg(C_KV, C_Z)
    kn = kn_ref[...]
    ksel = _rope_cols(_head_rms(kv[:, 256:384], kn[1:2], bd), cosf, sinf)
    kwin = _rope_cols(_head_rms(kv[:, 512:640], kn[2:3], bd), cosf, sinf)
    rows4 = jnp.concatenate([kv[:, 0:256], ksel, kv[:, 384:512]], axis=1)
    rows4_out[...] = rows4
    rows4t_out[0] = rows4.T
    rowsw_out[:, 0:128] = kwin
    rowsw_out[:, 128:256] = kv[:, 640:768]
    kvb_out[:, 0:128] = ksel.astype(BF16)
    kvb_out[:, 128:256] = kv[:, 384:512].astype(BF16)
    kvb_out[:, 256:384] = kwin.astype(BF16)
    kvb_out[:, 384:512] = kv[:, 640:768].astype(BF16)
    vt = jnp.concatenate([kv[:, 384:512], kv[:, 640:768]], axis=1).T
    vt_out[0, 0] = vt[0:128].astype(BF16)
    vt_out[0, 1] = vt[128:256].astype(BF16)

    z = seg(C_Z, C_U)
    zs_out[...] = z * _sigmoid(z)
    u_out[...] = seg(C_U, C_QM)
    qm = _head_rms(seg(C_QM, C_G), mqn_ref[...], bd) * (SCALE * LOG2E)
    qm_out[...] = qm.astype(BF16)
    g_out[...] = _sigmoid(seg(C_G, N_INP))


def _project(x2d, nb, t, lng, w_p, cos_t, sin_t, n_tab_blocks, qn, kn, mqn, bd, tb):
    n, d = x2d.shape
    nblk = t // tb
    row = lambda i: (i, 0)
    tab = lambda i: (i % n_tab_blocks, 0)
    outs = [
        jax.ShapeDtypeStruct((n, 1024), BF16),
        jax.ShapeDtypeStruct((n, 512), F32),
        jax.ShapeDtypeStruct((nb, 512, t), F32),
        jax.ShapeDtypeStruct((n, 256), F32),
        jax.ShapeDtypeStruct((n, 512), BF16),
        jax.ShapeDtypeStruct((nb, 2, LANES, t), BF16),
        jax.ShapeDtypeStruct((n, 1024), F32),
        jax.ShapeDtypeStruct((n, RWKV_SHIFT_W), F32),
        jax.ShapeDtypeStruct((n, 512), BF16),
        jax.ShapeDtypeStruct((n, 128), F32),
    ]
    out_specs = [pl.BlockSpec((tb, o.shape[1]), row) for o in outs]
    out_specs[2] = pl.BlockSpec((1, 512, tb), lambda i: (i // nblk, 0, i % nblk))
    out_specs[5] = pl.BlockSpec((1, 2, LANES, tb), lambda i: (i // nblk, 0, 0, i % nblk))
    return pl.pallas_call(
        _proj_body,
        grid=(n // tb,),
        in_specs=[
            pl.BlockSpec((tb, d), row),
            _full((1, d)),
            _full((d, N_INP)),
            pl.BlockSpec((tb, LANES), tab),
            pl.BlockSpec((tb, LANES), tab),
            _full((1, 1024)),
            _full((3, LANES)),
            _full((1, 512)),
            _full((LANES, LANES)),
        ],
        out_specs=out_specs,
        out_shape=outs,
        compiler_params=_cp(("arbitrary",)),
        name="proj",
    )(x2d, lng, w_p, cos_t, sin_t, qn, kn, mqn, bd)


def _compress_tail(lo, hi, n_ch, b1_ref, w2_ref, b2_ref, ckn_ref, cos_ref, sin_ref, bd_ref):
    his = pltpu.roll(hi, n_ch - 1, axis=0)
    pre = lo + his + b1_ref[...]
    hid = pre * _sigmoid(pre)
    out2 = _nn(hid.astype(BF16), w2_ref[...]) + b2_ref[...]
    ck = _head_rms(out2[:, 0:128], ckn_ref[...], bd_ref[...])
    ck = _rope_cols(ck, cos_ref[...], sin_ref[...])
    return ck, out2[:, 128:256]


def _compress_compute(xk_ref, xv_ref, n_ch, pelo_ref, pehi_ref, w1lo_ref, w1hi_ref, b1_ref, w2_ref, b2_ref,
                      ckn_ref, cos_ref, sin_ref, bd_ref):
    lo = jnp.zeros((n_ch, 256), F32)
    hi = jnp.zeros((n_ch, 256), F32)
    for p in range(CMP_STRIDE):
        xp = jnp.concatenate([xk_ref[pl.ds(p, n_ch, stride=CMP_STRIDE), :],
                              xv_ref[pl.ds(p, n_ch, stride=CMP_STRIDE), :]], axis=1)
        lo = lo + _nn((xp + pelo_ref[p:p + 1, :]).astype(BF16), w1lo_ref[p])
        hi = hi + _nn((xp + pehi_ref[p:p + 1, :]).astype(BF16), w1hi_ref[p])
    return _compress_tail(lo, hi, n_ch, b1_ref, w2_ref, b2_ref, ckn_ref, cos_ref, sin_ref, bd_ref)


def _compress_body(xk_ref, xv_ref, pelo_ref, pehi_ref, w1lo_ref, w1hi_ref, b1_ref, w2_ref, b2_ref,
                   ckn_ref, cos_ref, sin_ref, bd_ref, ck_out, cv_out, *, n_ch):
    ck, cv = _compress_compute(xk_ref, xv_ref, n_ch, pelo_ref, pehi_ref, w1lo_ref, w1hi_ref, b1_ref, w2_ref,
                               b2_ref, ckn_ref, cos_ref, sin_ref, bd_ref)
    ck_out[0] = ck.astype(BF16)
    cv_out[0] = cv.T.astype(BF16)


def _cmp_specs(n_ch):
    return [
        _full((CMP_STRIDE, 256)), _full((CMP_STRIDE, 256)),
        _full((CMP_STRIDE, 256, 256)), _full((CMP_STRIDE, 256, 256)),
        _full((1, 256)), _full((256, 256)), _full((1, 256)),
        _full((1, LANES)), _full((n_ch, LANES)), _full((n_ch, LANES)), _full((LANES, LANES)),
    ]


def _compress_prompt(rows4, cmp_args, b, t):
    n_ch = t // CMP_STRIDE
    outs = [jax.ShapeDtypeStruct((b, n_ch, LANES), BF16), jax.ShapeDtypeStruct((b, LANES, n_ch), BF16)]
    return pl.pallas_call(
        functools.partial(_compress_body, n_ch=n_ch),
        grid=(b,),
        in_specs=[pl.BlockSpec((t, LANES), lambda i: (i, 0)), pl.BlockSpec((t, LANES), lambda i: (i, 1))]
        + _cmp_specs(n_ch),
        out_specs=[pl.BlockSpec((1, n_ch, LANES), lambda i: (i, 0, 0)), pl.BlockSpec((1, LANES, n_ch), lambda i: (i, 0, 0))],
        out_shape=outs,
        compiler_params=_cp(("arbitrary",)),
        name="compress_prompt",
    )(rows4, rows4, *cmp_args)


def _select_blocks(score, n_sel):
    jj = lax.broadcasted_iota(jnp.int32, score.shape, 0)
    big = jnp.int32(score.shape[0])
    sel = jnp.zeros(score.shape, F32)
    for _ in range(n_sel):
        mx = jnp.max(score, axis=0, keepdims=True)
        first = jnp.min(jnp.where(score == mx, jj, big), axis=0, keepdims=True)
        pick = jj == first
        sel = jnp.where(pick, 1.0, sel)
        score = jnp.where(pick, -jnp.inf, score)
    return sel


def _nsa_prompt_body(q_ref, g_ref, kvb_ref, vt_ref, ck_ref, cvt_ref, ovt_ref, o_ref, sel_sc, *, n_cb, n_sb):
    i = pl.program_id(1)
    start = i * QBLK
    nh = NSA_HEADS
    q = q_ref[...]
    qall = jnp.concatenate([q[:, hd * LANES:(hd + 1) * LANES] for hd in range(nh)], axis=0)
    gT = g_ref[...].T
    qpos = start + lax.broadcasted_iota(jnp.int32, (1, QBLK), 1)
    qpos2 = jnp.concatenate([qpos, qpos], axis=1)
    wk = WINDOW + QBLK
    n_tiles = (start + QBLK + KTILE - 1) // KTILE
    lo_w = pl.multiple_of(jnp.maximum(start - WINDOW, 0), QBLK)

    cend = lax.broadcasted_iota(jnp.int32, (n_cb, 1), 0) * CMP_STRIDE + (CMP_BLK - 1)
    bias_c = jnp.where(cend <= qpos, 0.0, NEG)
    p = _softmax2_cols(_nt(ck_ref[0], qall), bias_c, qpos >= CMP_BLK - 1)
    cvt = cvt_ref[0]
    ocT, psums = [], []
    for g in range(NSA_KV_HEADS):
        pg = p[:, g * 512:(g + 1) * 512]
        ocT.append(_nn(cvt[g * 64:(g + 1) * 64], pg.astype(BF16)))
        psums.append(pg[:, 0:128] + pg[:, 128:256] + pg[:, 256:384] + pg[:, 384:512])
    ph, plo = _split2(jnp.concatenate(psums, axis=1))
    imp = _nn(ovt_ref[...], ph) + _nn(ovt_ref[...], plo)

    jj = lax.broadcasted_iota(jnp.int32, (n_sb, 2 * QBLK), 0)
    cur = lax.shift_right_logical(qpos2, 6)
    forced = (jj == 0) | (jj == cur) | (jj == cur - 1)
    score = jnp.where(jj * SEL_BLK <= qpos2, jnp.where(forced, jnp.inf, imp), -jnp.inf)
    sel_sc[...] = _select_blocks(score, N_SEL)

    bpt = KTILE // SEL_BLK

    def tile_step(t, carry):
        m, l, acc0, acc1 = carry
        k0 = pl.multiple_of(t * KTILE, KTILE)
        s = _nt(kvb_ref[pl.ds(k0, KTILE), 0:128], qall)
        sel8 = sel_sc[pl.ds(pl.multiple_of(t * bpt, bpt), bpt), :]
        member = jnp.concatenate([jnp.broadcast_to(sel8[b:b + 1, :], (SEL_BLK, 2 * QBLK)) for b in range(bpt)], axis=0)
        kpos = k0 + lax.broadcasted_iota(jnp.int32, (KTILE, 1), 0)
        bias = jnp.where((member > 0.5) & (kpos <= qpos2), 0.0, NEG)
        vT = vt_ref[0, 0, :, pl.ds(k0, KTILE)]
        ms, ls, als, ps = [], [], [], []
        for hd in range(nh):
            g = hd // NSA_HPG
            cs = slice(hd * LANES, (hd + 1) * LANES)
            sm = s[:, cs] + bias[:, g * LANES:(g + 1) * LANES]
            mn = jnp.maximum(m[:, cs], jnp.max(sm, axis=0, keepdims=True))
            alpha = jnp.exp2(m[:, cs] - mn)
            pp = jnp.exp2(sm - mn)
            ls.append(alpha * l[:, cs] + jnp.sum(pp, axis=0, keepdims=True))
            ms.append(mn)
            als.append(alpha)
            ps.append(pp.astype(BF16))
        acc0 = jnp.concatenate(als[0:4], axis=1) * acc0 + _nn(vT[0:64], jnp.concatenate(ps[0:4], axis=1))
        acc1 = jnp.concatenate(als[4:8], axis=1) * acc1 + _nn(vT[64:128], jnp.concatenate(ps[4:8], axis=1))
        return jnp.concatenate(ms, axis=1), jnp.concatenate(ls, axis=1), acc0, acc1

    m0 = jnp.full((1, nh * QBLK), NEG, F32)
    l0 = jnp.zeros((1, nh * QBLK), F32)
    a0 = jnp.zeros((HEAD_DIM, 4 * QBLK), F32)
    _, l, acc0, acc1 = lax.fori_loop(0, n_tiles, tile_step, (m0, l0, a0, a0))
    rl = 1.0 / l
    osT = [acc0 * rl[:, 0:512], acc1 * rl[:, 512:1024]]

    kpos_w = lo_w + lax.broadcasted_iota(jnp.int32, (wk, 1), 0)
    bias_w = jnp.where((kpos_w <= qpos) & (kpos_w > qpos - WINDOW), 0.0, NEG)
    pw = _softmax2_cols(_nt(kvb_ref[pl.ds(lo_w, wk), 256:384], qall), bias_w).astype(BF16)
    vwT = vt_ref[0, 1, :, pl.ds(lo_w, wk)]
    owT = [_nn(vwT[0:64], pw[:, 0:512]), _nn(vwT[64:128], pw[:, 512:1024])]

    cols = []
    for c in range(4):
        g = c // 2
        parts = []
        for hd in (2 * c, 2 * c + 1):
            cs = slice((hd % 4) * QBLK, (hd % 4 + 1) * QBLK)
            parts.append(gT[3 * hd:3 * hd + 1, :] * ocT[g][:, cs] + gT[3 * hd + 1:3 * hd + 2, :] * osT[g][:, cs]
                         + gT[3 * hd + 2:3 * hd + 3, :] * owT[g][:, cs])
        cols.append(jnp.concatenate(parts, axis=0).T)
    o_ref[...] = jnp.concatenate(cols, axis=1)


def _nsa_prompt(q, gates, kvb, vt, ck, cvt, ovt, b, t):
    nq = t // QBLK
    n_cb = ck.shape[1]
    n_sb = ovt.shape[0]
    return pl.pallas_call(
        functools.partial(_nsa_prompt_body, n_cb=n_cb, n_sb=n_sb),
        grid=(b, nq),
        in_specs=[
            pl.BlockSpec((QBLK, 1024), lambda bi, i: (bi * nq + i, 0)),
            pl.BlockSpec((QBLK, LANES), lambda bi, i: (bi * nq + i, 0)),
            pl.BlockSpec((t, 512), lambda bi, i: (bi, 0)),
            pl.BlockSpec((1, 2, LANES, t), lambda bi, i: (bi, 0, 0, 0)),
            pl.BlockSpec((1, n_cb, LANES), lambda bi, i: (bi, 0, 0)),
            pl.BlockSpec((1, LANES, n_cb), lambda bi, i: (bi, 0, 0)),
            _full(ovt.shape),
        ],
        out_specs=pl.BlockSpec((QBLK, 512), lambda bi, i: (bi * nq + i, 0)),
        out_shape=jax.ShapeDtypeStruct((b * t, 512), F32),
        scratch_shapes=[pltpu.VMEM((n_sb, 2 * QBLK), F32)],
        compiler_params=_cp(("arbitrary", "arbitrary")),
        name="nsa_prompt",
    )(q, gates, kvb, vt, ck, cvt, ovt)


def _rwkv_prep(u, mu, w0, w2, a0, a2, kkp, kap, bd):
    r = u[:, 0:256]
    k = u[:, 256:512]
    v = u[:, 512:768]
    ww = w0 + _nn_x(jnp.tanh(u[:, 768:832]), w2)
    nw = -ww
    sp = jnp.maximum(nw, 0.0) + jnp.log(1.0 + jnp.exp(-jnp.abs(nw)))
    logdec = -jnp.exp(-sp - 0.5)
    a = _sigmoid(a0 + _nn_x(u[:, 832:896], a2))
    kk = k * kkp
    kk = kk / jnp.maximum(jnp.sqrt(_segsum(kk * kk, bd)), 1e-12)
    k2 = k * (1.0 + (a - 1.0) * kap)
    return r, k2, v, kk, kk * a, logdec


def _bmm_nt(a, b):
    return lax.dot_general(a, b, (((2,), (2,)), ((0,), (0,))), preferred_element_type=F32)


def _bmm(a, b):
    return lax.dot_general(a, b, (((2,), (1,)), ((0,), (0,))), preferred_element_type=F32)


def _rwkv_body(u_ref, mu_ref, w0_ref, w2_ref, a0_ref, a2_ref, kk_ref, ka_ref, rk_ref,
               lnw_ref, lnb_ref, bd_ref, o_ref, sout_ref, s_sc, prev_sc, *, nbat, n_dbl):
    c = pl.program_id(1)
    C = RWKV_CHUNK
    D = HEAD_DIM
    R = nbat * C
    G = nbat * RWKV_HEADS

    @pl.when(c == 0)
    def _():
        s_sc[...] = jnp.zeros(s_sc.shape, F32)
        prev_sc[...] = jnp.zeros(prev_sc.shape, F32)

    bd = bd_ref[...]
    rowi = lax.broadcasted_iota(jnp.int32, (C, 1), 0)
    us = []
    for n in range(nbat):
        ucur = u_ref[n]
        prev = jnp.where(rowi == 0, prev_sc[n], pltpu.roll(ucur, 1, axis=0))
        prev_sc[n] = ucur[C - 1:C, :]
        us.append(ucur + (prev - ucur) * mu_ref[...])
    u = us[0] if nbat == 1 else jnp.concatenate(us, axis=0)
    r, k2, v, kk, bb, logdec = _rwkv_prep(u, mu_ref[...], w0_ref[...], w2_ref[...], a0_ref[...], a2_ref[...],
                                          kk_ref[...], ka_ref[...], bd)

    ri = lax.broadcasted_iota(jnp.int32, (R, R), 0)
    ci = lax.broadcasted_iota(jnp.int32, (R, R), 1)
    same = lax.shift_right_logical(ri, 6) == lax.shift_right_logical(ci, 6)
    tri = jnp.where(same & (ri >= ci), 1.0, 0.0).astype(BF16)
    ones = jnp.where(same, 1.0, 0.0).astype(BF16)
    d1, d2, d3 = _split3(logdec)
    lcum = _nn(tri, d1) + _nn(tri, d2) + _nn(tri, d3)
    lend = _nn(ones, d1) + _nn(ones, d2) + _nn(ones, d3)
    e_neg = jnp.exp(-lcum)
    e_end = jnp.exp(lend - lcum)
    gend = jnp.exp(lend)

    def grp(x):
        return jnp.stack([x[n * C:(n + 1) * C, h * D:(h + 1) * D] for n in range(nbat) for h in range(RWKV_HEADS)])

    ar = jnp.concatenate([grp(-kk * jnp.exp(lcum - logdec)), grp(r * jnp.exp(lcum))], axis=1).astype(BF16)
    bk = jnp.concatenate([grp(bb * e_neg), grp(k2 * e_neg)], axis=1).astype(BF16)
    bkg = jnp.concatenate([grp(bb * e_end), grp(k2 * e_end)], axis=1).astype(BF16)
    vg = grp(v).astype(BF16)
    gg = jnp.stack([gend[n * C:n * C + 1, h * D:(h + 1) * D] for n in range(nbat) for h in range(RWKV_HEADS)])

    ti = lax.broadcasted_iota(jnp.int32, (1, C, C), 1)
    ii = lax.broadcasted_iota(jnp.int32, (1, C, C), 2)
    strict = ti > ii
    incl = ti >= ii
    s_old = s_sc[...]
    mm = _bmm_nt(ar, bk)
    lab = jnp.where(strict, mm[:, 0:C, 0:C], 0.0)
    lak = jnp.where(strict, mm[:, 0:C, C:2 * C], 0.0)
    mrb = jnp.where(incl, mm[:, C:2 * C, 0:C], 0.0)
    mrk = jnp.where(incl, mm[:, C:2 * C, C:2 * C], 0.0)
    ars = _bmm_nt(ar, s_old.astype(BF16))
    rhs = ars[:, 0:C] + _bmm(lak.astype(BF16), vg)
    x = jnp.where(ti == ii, 1.0, 0.0) + lab
    pw = lab
    for _ in range(n_dbl):
        pwb = pw.astype(BF16)
        pw = _bmm(pwb, pwb)
        x = x + _bmm(pw.astype(BF16), x.astype(BF16))
    uu = _bmm(x.astype(BF16), rhs.astype(BF16))
    yg = ars[:, C:2 * C] + _bmm(mrb.astype(BF16), uu.astype(BF16)) + _bmm(mrk.astype(BF16), vg)
    uv = jnp.concatenate([uu.astype(BF16), vg], axis=1)
    eye_g = jnp.broadcast_to(_eye(D)[None], (G, D, D))
    uvt = _bmm_nt(eye_g, uv).astype(BF16)
    s_sc[...] = s_old * gg + _bmm(uvt, bkg)

    y = jnp.concatenate([jnp.concatenate([yg[n * RWKV_HEADS + h] for h in range(RWKV_HEADS)], axis=1)
                         for n in range(nbat)], axis=0)
    mean = _segsum(y, bd) * (1.0 / D)
    dy = y - mean
    var = _segsum(dy * dy, bd) * (1.0 / D)
    yn = dy * lax.rsqrt(var + GN_EPS) * lnw_ref[...] + lnb_ref[...]
    out = yn + _segsum(r * k2 * rk_ref[...], bd) * v
    for n in range(nbat):
        o_ref[n] = out[n * C:(n + 1) * C]

    @pl.when(c == pl.num_programs(1) - 1)
    def _():
        for n in range(nbat):
            sout_ref[n] = s_sc[n * RWKV_HEADS:(n + 1) * RWKV_HEADS]


def _rwkv_prompt(u3, rw_args, bd, nbat):
    b, t, _ = u3.shape
    C = RWKV_CHUNK
    n_dbl = int(np.log2(C)) - 1
    outs = [jax.ShapeDtypeStruct((b, t, RWKV_W), F32),
            jax.ShapeDtypeStruct((b, RWKV_HEADS, HEAD_DIM, HEAD_DIM), F32)]
    vec = _full((1, RWKV_W))
    return pl.pallas_call(
        functools.partial(_rwkv_body, nbat=nbat, n_dbl=n_dbl),
        grid=(b // nbat, t // C),
        in_specs=[
            pl.BlockSpec((nbat, C, RWKV_SHIFT_W), lambda bi, c: (bi, c, 0)),
            _full((1, RWKV_SHIFT_W)),
            vec, _full((HEAD_DIM, RWKV_W)), vec, _full((HEAD_DIM, RWKV_W)),
            vec, vec, vec, vec, vec, _full((LANES, LANES)),
        ],
        out_specs=[pl.BlockSpec((nbat, C, RWKV_W), lambda bi, c: (bi, c, 0)),
                   pl.BlockSpec((nbat, RWKV_HEADS, HEAD_DIM, HEAD_DIM), lambda bi, c: (bi, 0, 0, 0))],
        out_shape=outs,
        scratch_shapes=[pltpu.VMEM((nbat * RWKV_HEADS, HEAD_DIM, HEAD_DIM), F32),
                        pltpu.VMEM((nbat, 1, RWKV_SHIFT_W), F32)],
        compiler_params=_cp(("arbitrary", "arbitrary")),
        name="rwkv_prompt",
    )(u3, *rw_args, bd)


def _rwkv_dec_body(u_ref, sh_ref, st_ref, mu_ref, w0_ref, w2t_ref, a0_ref, a2t_ref, kk_ref, ka_ref, rk_ref,
                   lnw_ref, lnb_ref, o_ref, so_ref, a_sc, w_sc, b_sc, k_sc, r_sc, v_sc, y_sc):
    h = pl.program_id(0)
    nh = pl.num_programs(0)
    D = HEAD_DIM

    @pl.when(h == 0)
    def _():
        uT = u_ref[...].T
        pT = sh_ref[...].T
        xT = uT + (pT - uT) * mu_ref[...]
        rT = xT[0:256]
        kT = xT[256:512]
        ww = w0_ref[...] + _nn_x(w2t_ref[...], jnp.tanh(xT[768:832]))
        nw = -ww
        sp = jnp.maximum(nw, 0.0) + jnp.log(1.0 + jnp.exp(-jnp.abs(nw)))
        w_sc[...] = jnp.exp(-jnp.exp(-sp - 0.5))
        aT = _sigmoid(a0_ref[...] + _nn_x(a2t_ref[...], xT[832:896]))
        kk = kT * kk_ref[...]
        for hh in range(RWKV_HEADS):
            blk = kk[hh * D:(hh + 1) * D]
            nrm = jnp.maximum(jnp.sqrt(jnp.sum(blk * blk, axis=0, keepdims=True)), 1e-12)
            a_sc[hh * D:(hh + 1) * D, :] = -(blk / nrm)
        b_sc[...] = -a_sc[...] * aT
        k_sc[...] = kT * (1.0 + (aT - 1.0) * ka_ref[...])
        r_sc[...] = rT
        v_sc[...] = xT[512:768]

    base = pl.multiple_of(h * D, D)
    at = a_sc[pl.ds(base, D), :]
    wt = w_sc[pl.ds(base, D), :]
    bt = b_sc[pl.ds(base, D), :]
    kt = k_sc[pl.ds(base, D), :]
    rt = r_sc[pl.ds(base, D), :]

    def vstep(vi, carry):
        s_v = st_ref[0, vi]
        sa = jnp.sum(s_v * at, axis=0, keepdims=True)
        vv = v_sc[pl.ds(base + vi, 1), :]
        s_n = s_v * wt + sa * bt + vv * kt
        so_ref[0, vi] = s_n
        y_sc[pl.ds(base + vi, 1), :] = jnp.sum(s_n * rt, axis=0, keepdims=True)
        return carry

    lax.fori_loop(0, D, vstep, 0)

    @pl.when(h == nh - 1)
    def _():
        outs = []
        for hh in range(RWKV_HEADS):
            sl = slice(hh * D, (hh + 1) * D)
            y = y_sc[sl, :]
            mean = jnp.mean(y, axis=0, keepdims=True)
            dy = y - mean
            var = jnp.mean(dy * dy, axis=0, keepdims=True)
            yn = dy * lax.rsqrt(var + GN_EPS) * lnw_ref[sl, :] + lnb_ref[sl, :]
            bonus = jnp.sum(r_sc[sl, :] * k_sc[sl, :] * rk_ref[sl, :], axis=0, keepdims=True) * v_sc[sl, :]
            outs.append(yn + bonus)
        o_ref[...] = jnp.concatenate(outs, axis=0).T


def _rwkv_decode(u_s, shift, state_t, rw_cols):
    s = u_s.shape[0]
    D = HEAD_DIM
    col = lambda n: _full((n, 1))
    vec_sc = pltpu.VMEM((RWKV_W, s), F32)
    return pl.pallas_call(
        _rwkv_dec_body,
        grid=(RWKV_HEADS,),
        in_specs=[_full((s, RWKV_SHIFT_W)), _full((s, RWKV_SHIFT_W)),
                  pl.BlockSpec((1, D, D, s), lambda h: (h, 0, 0, 0)),
                  col(RWKV_SHIFT_W), col(RWKV_W), _full((RWKV_W, D)), col(RWKV_W), _full((RWKV_W, D)),
                  col(RWKV_W), col(RWKV_W), col(RWKV_W), col(RWKV_W), col(RWKV_W)],
        out_specs=[_full((s, RWKV_W)), pl.BlockSpec((1, D, D, s), lambda h: (h, 0, 0, 0))],
        out_shape=[jax.ShapeDtypeStruct((s, RWKV_W), F32), jax.ShapeDtypeStruct((RWKV_HEADS, D, D, s), F32)],
        scratch_shapes=[vec_sc] * 7,
        compiler_params=_cp(("arbitrary",)),
        name="rwkv_decode",
    )(u_s, shift, state_t, *rw_cols)


def _memkv_body(m_ref, g_ref, w_ref, kn_ref, bd_ref, o_ref):
    x = m_ref[0]
    ms = jnp.mean(x * x, axis=-1, keepdims=True)
    xn = (x * lax.rsqrt(ms + RMS_EPS) * g_ref[...]).astype(BF16)
    kv = _nn(xn, w_ref[...])
    o_ref[0, :, 0:256] = _head_rms(kv[:, 0:256], kn_ref[...], bd_ref[...])
    o_ref[0, :, 256:512] = kv[:, 256:512]


def _mem_kv(mem, g, w, kn, bd):
    b, m, d = mem.shape
    return pl.pallas_call(
        _memkv_body,
        grid=(b,),
        in_specs=[pl.BlockSpec((1, m, d), lambda i: (i, 0, 0)), _full((1, d)), _full((d, 512)),
                  _full((1, 256)), _full((LANES, LANES))],
        out_specs=pl.BlockSpec((1, m, 512), lambda i: (i, 0, 0)),
        out_shape=jax.ShapeDtypeStruct((b, m, 512), F32),
        compiler_params=_cp(("arbitrary",)),
        name="mem_kv",
    )(mem, g, w, kn, bd)


def _mem_attn_body(qm_ref, kv_ref, o_ref):
    qm = qm_ref[...]
    kvb = kv_ref[0].astype(BF16)
    eye = _eye(LANES)
    heads = []
    for h in range(MEM_HEADS):
        col = h // 2
        kc = kvb[:, col * LANES:(col + 1) * LANES]
        vc = kvb[:, 256 + col * LANES:256 + (col + 1) * LANES]
        s = _nt(kc, qm[:, h * LANES:(h + 1) * LANES])
        m = jnp.max(s, axis=0, keepdims=True)
        p = jnp.exp2(s - m)
        p = p * (1.0 / jnp.sum(p, axis=0, keepdims=True))
        oT = _nn(_nt(eye, vc).astype(BF16), p.astype(BF16))
        heads.append(oT.T)
    lane = lax.broadcasted_iota(jnp.int32, heads[0].shape, 1)
    o_ref[...] = jnp.concatenate([jnp.where(lane < 64, heads[0], heads[1]),
                                  jnp.where(lane < 64, heads[2], heads[3])], axis=1)


def _mem_attn(qm, memkv, nb, rows_per_b, tb):
    n = qm.shape[0]
    nblk = rows_per_b // tb
    m = memkv.shape[1]
    return pl.pallas_call(
        _mem_attn_body,
        grid=(nb, nblk),
        in_specs=[pl.BlockSpec((tb, 512), lambda b, i: (b * nblk + i, 0)),
                  pl.BlockSpec((1, m, 512), lambda b, i: (b, 0, 0))],
        out_specs=pl.BlockSpec((tb, MEM_W), lambda b, i: (b * nblk + i, 0)),
        out_shape=jax.ShapeDtypeStruct((n, MEM_W), F32),
        compiler_params=_cp(("arbitrary", "arbitrary")),
        name="mem_attn",
    )(qm, memkv)


def _out_body(x_ref, on_ref, or_ref, om_ref, zs_ref, w_ref, y_ref):
    zs = zs_ref[...]
    cat = jnp.concatenate([on_ref[...] * zs[:, 0:512], or_ref[...] * zs[:, 512:768],
                           om_ref[...] * zs[:, 768:1024]], axis=1)
    y_ref[...] = x_ref[...] + _nn(cat.astype(BF16), w_ref[...])


def _out_proj(x2d, o_nsa, o_rw, o_mem, zs, w_out, tb):
    n, d = x2d.shape
    row = lambda i: (i, 0)
    return pl.pallas_call(
        _out_body,
        grid=(n // tb,),
        in_specs=[pl.BlockSpec((tb, d), row), pl.BlockSpec((tb, 512), row), pl.BlockSpec((tb, 256), row),
                  pl.BlockSpec((tb, 256), row), pl.BlockSpec((tb, 1024), row), _full((1024, d))],
        out_specs=pl.BlockSpec((tb, d), row),
        out_shape=jax.ShapeDtypeStruct((n, d), F32),
        compiler_params=_cp(("arbitrary",)),
        name="out_proj",
    )(x2d, o_nsa, o_rw, o_mem, zs, w_out)


PAGES_PER_STEP = 8


def _query_rows(qrow, g):
    rows = lax.broadcasted_iota(jnp.int32, (16, LANES), 0)
    qg = jnp.zeros((16, LANES), F32)
    for h in range(4):
        piece = qrow[:, (4 * g + h) * LANES:(4 * g + h + 1) * LANES].astype(F32)
        qg = jnp.where(rows == h, piece, qg)
    return qg.astype(BF16)


def _dec_a_body(pt_ref, *refs, n_ch, n_sb, n_sbp, p_len):
    pages = refs[:PAGES_PER_STEP]
    (q_ref, pet_ref, perm_ref, w1lo_ref, w1hi_ref, b1_ref, w2_ref, b2_ref, ckn_ref, cos_ref, sin_ref,
     bd_ref, ov_ref, oc_out, idx_out, xlo, xhi) = refs[PAGES_PER_STEP:]
    j = pl.program_id(1)
    cpp = PAGE_SIZE // CMP_STRIDE
    perm = perm_ref[...]
    for e in range(2):
        for half, dst in ((0, xlo), (1, xhi)):
            stacked = jnp.concatenate([(pages[i][0, e] + pet_ref[half, e]).astype(BF16) for i in range(PAGES_PER_STEP)],
                                      axis=0)
            y = _nt(perm, stacked)
            for i in range(PAGES_PER_STEP):
                c0 = pl.multiple_of((j * PAGES_PER_STEP + i) * cpp, cpp)
                for p in range(CMP_STRIDE):
                    col = p * 256 + e * LANES
                    dst[pl.ds(c0, cpp), col:col + LANES] = y[p * cpp:(p + 1) * cpp, i * LANES:(i + 1) * LANES]

    @pl.when(j == pl.num_programs(1) - 1)
    def _():
        lo = _nn(xlo[...].astype(BF16), w1lo_ref[...])
        hi = _nn(xhi[...].astype(BF16), w1hi_ref[...])
        ck, cv = _compress_tail(lo, hi, n_ch, b1_ref, w2_ref, b2_ref, ckn_ref, cos_ref, sin_ref, bd_ref)
        ckb = ck.astype(BF16)
        cvb = cv.astype(BF16)
        qrow = q_ref[0]
        cend = lax.broadcasted_iota(jnp.int32, (1, n_ch), 1) * CMP_STRIDE + (CMP_BLK - 1)
        valid = cend <= p_len
        ocs = []
        imps = []
        for g in range(NSA_KV_HEADS):
            s = _nt(_query_rows(qrow, g), ckb)
            sm = jnp.where(valid, s, NEG)
            m = jnp.max(sm, axis=1, keepdims=True)
            p = jnp.where(valid, jnp.exp2(sm - m), 0.0)
            p = p * (1.0 / jnp.maximum(jnp.sum(p, axis=1, keepdims=True), 1e-30))
            ocs.append(_nn(p.astype(BF16), cvb))
            imps.append(p[0:1] + p[1:2] + p[2:3] + p[3:4])
        oc_out[0] = jnp.concatenate(ocs, axis=1)
        pz = jnp.concatenate(imps + [jnp.zeros((6, n_ch), F32)], axis=0)
        ph, plo = _split2(pz)
        imp = _nn(ph, ov_ref[...]) + _nn(plo, ov_ref[...])
        jj = lax.broadcasted_iota(jnp.int32, (8, n_sbp), 1)
        cur = p_len // SEL_BLK
        forced = (jj == 0) | (jj == cur) | (jj == cur - 1)
        score = jnp.where(jj < n_sb, jnp.where(forced, jnp.inf, imp), -jnp.inf)
        lane = lax.broadcasted_iota(jnp.int32, (8, LANES), 1)
        idx = jnp.zeros((8, LANES), jnp.int32)
        for n in range(N_SEL):
            mx = jnp.max(score, axis=1, keepdims=True)
            first = jnp.min(jnp.where(score == mx, jj, n_sbp), axis=1, keepdims=True)
            idx = jnp.where(lane == n, first, idx)
            score = jnp.where(jj == first, -jnp.inf, score)
        idx_out[0] = idx


def _decode_a(pt, pool_t, q3, pet, perm, cmp_args, ov, s, p_len):
    n_pages = p_len // PAGE_SIZE
    n_ch = p_len // CMP_STRIDE
    n_sb = p_len // SEL_BLK + 1
    n_sbp = ov.shape[1]
    steps = n_pages // PAGES_PER_STEP

    def page_spec(i):
        return pl.BlockSpec((1, 2, LANES, PAGE_SIZE),
                            lambda si, j, pt_ref: (pt_ref[si, j * PAGES_PER_STEP + i], 0, 0, 0))

    full = lambda shape: pl.BlockSpec(shape, lambda si, j, pt_ref: (0,) * len(shape))
    cmp_specs = [full((2, 2, LANES, PAGE_SIZE)), full((PAGE_SIZE, PAGE_SIZE)), full((CMP_STRIDE * 256, 256)),
                 full((CMP_STRIDE * 256, 256)), full((1, 256)), full((256, 256)), full((1, 256)),
                 full((1, LANES)), full((n_ch, LANES)), full((n_ch, LANES)), full((LANES, LANES))]
    grid_spec = pltpu.PrefetchScalarGridSpec(
        num_scalar_prefetch=1,
        grid=(s, steps),
        in_specs=[page_spec(i) for i in range(PAGES_PER_STEP)]
        + [pl.BlockSpec((1, 1, 1024), lambda si, j, pt_ref: (si, 0, 0))] + cmp_specs + [full(ov.shape)],
        out_specs=[pl.BlockSpec((1, 16, 256), lambda si, j, pt_ref: (si, 0, 0)),
                   pl.BlockSpec((1, 8, LANES), lambda si, j, pt_ref: (si, 0, 0))],
        scratch_shapes=[pltpu.VMEM((n_ch, CMP_STRIDE * 256), F32), pltpu.VMEM((n_ch, CMP_STRIDE * 256), F32)],
    )
    return pl.pallas_call(
        functools.partial(_dec_a_body, n_ch=n_ch, n_sb=n_sb, n_sbp=n_sbp, p_len=p_len),
        grid_spec=grid_spec,
        out_shape=[jax.ShapeDtypeStruct((s, 16, 256), F32), jax.ShapeDtypeStruct((s, 8, LANES), jnp.int32)],
        compiler_params=_cp(("arbitrary", "arbitrary")),
        name="decode_select",
    )(pt, *([pool_t] * PAGES_PER_STEP), q3, pet, perm, cmp_args[2].reshape(CMP_STRIDE * 256, 256),
      cmp_args[3].reshape(CMP_STRIDE * 256, 256), *cmp_args[4:], ov)


def _dec_b_body(pt_ref, idx_ref, pool_ref, q_ref, g_ref, oc_ref, new4_ref, neww_ref, win_ref, o_out, win_out,
                buf, sem, kall, vall, *, n_blk):
    nsel = NSA_KV_HEADS * N_SEL
    si = pl.program_id(0)
    n_seq = pl.num_programs(0)
    slot = lax.rem(si, 2)

    def page_copy(s, n, sl):
        jb = jnp.minimum(idx_ref[s, n], n_blk - 1)
        page = pt_ref[s, lax.shift_right_logical(jb, 1)]
        return pltpu.make_async_copy(pool_ref.at[page, pl.ds(2, 2)], buf.at[sl, n], sem.at[sl])

    @pl.when(si == 0)
    def _():
        for n in range(nsel):
            page_copy(0, n, 0).start()

    @pl.when(si + 1 < n_seq)
    def _():
        for n in range(nsel):
            page_copy(si + 1, n, 1 - slot).start()

    for n in range(nsel):
        page_copy(si, n, slot).wait()

    qrow = q_ref[0]
    gates = g_ref[0]
    oc = oc_ref[0]
    new4 = new4_ref[0]
    neww = neww_ref[0]

    wlen = win_ref.shape[3]
    r0 = lax.broadcasted_iota(jnp.int32, (LANES, LANES), 0)
    r1 = lax.broadcasted_iota(jnp.int32, (LANES, LANES), 1)
    lane_w = lax.broadcasted_iota(jnp.int32, (LANES, wlen), 1)
    new_t = []
    for c in range(2):
        col = jnp.sum(jnp.where(r0 == r1, neww[:, c * LANES:(c + 1) * LANES], 0.0), axis=1, keepdims=True)
        shifted = pltpu.roll(win_ref[0, c], wlen - 1, axis=1)
        nt = jnp.where(lane_w == wlen - 1, col, shifted)
        win_out[0, c] = nt
        new_t.append(nt.astype(BF16))
    kwT, vwT = new_t

    ksel_new = new4[:, 256:384].astype(BF16).astype(F32)
    vsel_new = new4[:, 384:512].astype(BF16).astype(F32)
    nk = N_SEL * PAGE_SIZE
    lane_k = lax.broadcasted_iota(jnp.int32, (16, nk), 1)

    heads = []
    for g in range(NSA_KV_HEADS):
        qgb = _query_rows(qrow, g)
        valid = lane_k < 0
        for n in range(N_SEL):
            kall[:, n * PAGE_SIZE:(n + 1) * PAGE_SIZE] = buf[slot, g * N_SEL + n, 0].astype(BF16)
            vall[:, n * PAGE_SIZE:(n + 1) * PAGE_SIZE] = buf[slot, g * N_SEL + n, 1].astype(BF16)
            jb = idx_ref[si, g * N_SEL + n]
            half = jb % 2
            valid = valid | ((jb < n_blk) & (lax.shift_right_logical(lane_k, 7) == n)
                             & ((lax.shift_right_logical(lane_k, 6) & 1) == half))
        s = _nn(qgb, kall[...])
        s_new = jnp.sum(qgb.astype(F32) * ksel_new, axis=1, keepdims=True)
        sm = jnp.where(valid, s, NEG)
        m = jnp.maximum(jnp.max(sm, axis=1, keepdims=True), s_new)
        p = jnp.where(valid, jnp.exp2(sm - m), 0.0)
        p_new = jnp.exp2(s_new - m)
        l = jnp.sum(p, axis=1, keepdims=True) + p_new
        o_s = (_nt(p.astype(BF16), vall[...]) + p_new.astype(BF16).astype(F32) * vsel_new) * (1.0 / l)

        sw = _nn(qgb, kwT)
        mw = jnp.max(sw, axis=1, keepdims=True)
        pw = jnp.exp2(sw - mw)
        o_w = _nt(pw.astype(BF16), vwT) * (1.0 / jnp.sum(pw, axis=1, keepdims=True))

        o_c = oc[:, g * LANES:(g + 1) * LANES]
        for h in range(4):
            hd = 4 * g + h
            mix = (gates[:, 3 * hd:3 * hd + 1] * o_c[h:h + 1, :] + gates[:, 3 * hd + 1:3 * hd + 2] * o_s[h:h + 1, :]
                   + gates[:, 3 * hd + 2:3 * hd + 3] * o_w[h:h + 1, :])
            if hd % 2 != g:
                mix = pltpu.roll(mix, 64, axis=1)
            heads.append(mix)
    lane1 = lax.broadcasted_iota(jnp.int32, (1, LANES), 1)
    cols = [jnp.where(lane1 < 64, heads[2 * c], heads[2 * c + 1]) for c in range(4)]
    o_out[0] = jnp.concatenate(cols, axis=1)


def _decode_b(pt, idx32, pool_t, q3, g3, oc, new4, neww, win_t, s, p_len):
    n_blk = p_len // SEL_BLK
    nsel = NSA_KV_HEADS * N_SEL
    wlen = win_t.shape[3]

    per = lambda shape: pl.BlockSpec(shape, lambda si, pt_ref, idx_ref: (si,) + (0,) * (len(shape) - 1))
    grid_spec = pltpu.PrefetchScalarGridSpec(
        num_scalar_prefetch=2,
        grid=(s,),
        in_specs=[pl.BlockSpec(memory_space=pl.ANY),
                  per((1, 1, 1024)), per((1, 1, LANES)), per((1, 16, 256)), per((1, 1, 512)), per((1, 1, 256)),
                  per((1, 2, LANES, wlen))],
        out_specs=[per((1, 1, 512)), per((1, 2, LANES, wlen))],
        scratch_shapes=[pltpu.VMEM((2, nsel, 2, LANES, PAGE_SIZE), F32), pltpu.SemaphoreType.DMA((2,)),
                        pltpu.VMEM((LANES, N_SEL * PAGE_SIZE), BF16), pltpu.VMEM((LANES, N_SEL * PAGE_SIZE), BF16)],
    )
    return pl.pallas_call(
        functools.partial(_dec_b_body, n_blk=n_blk),
        grid_spec=grid_spec,
        out_shape=[jax.ShapeDtypeStruct((s, 1, 512), F32), jax.ShapeDtypeStruct((s, 2, LANES, wlen), F32)],
        compiler_params=_cp(("arbitrary",)),
        name="decode_attend",
    )(pt, idx32, pool_t, q3, g3, oc, new4, neww, win_t)


def _mem_dec_body(qm_ref, kv_ref, o_ref):
    qrow = qm_ref[0]
    kT = kv_ref[0, 0].astype(BF16)
    vT = kv_ref[0, 1].astype(BF16)
    rows = lax.broadcasted_iota(jnp.int32, (16, 256), 0)
    lanes = lax.broadcasted_iota(jnp.int32, (16, 256), 1)
    qr = jnp.zeros((16, 256), F32)
    for h in range(MEM_HEADS):
        piece = qrow[:, h * LANES:(h + 1) * LANES].astype(F32)
        col = h // 2
        wide = jnp.concatenate([piece if c == col else jnp.zeros_like(piece) for c in range(2)], axis=1)
        qr = jnp.where(rows == h, wide, qr)
    s = _nn(qr.astype(BF16), kT)
    m = jnp.max(s, axis=1, keepdims=True)
    p = jnp.exp2(s - m)
    p = p * (1.0 / jnp.sum(p, axis=1, keepdims=True))
    of = _nt(p.astype(BF16), vT)
    keep = rows == lax.shift_right_logical(lanes, 6)
    o_ref[0] = jnp.sum(jnp.where(keep, of, 0.0), axis=0, keepdims=True)


def _mem_decode(qm3, mem_t):
    s, _, hd, m = mem_t.shape
    per = lambda shape: pl.BlockSpec(shape, lambda i: (i,) + (0,) * (len(shape) - 1))
    return pl.pallas_call(
        _mem_dec_body,
        grid=(s,),
        in_specs=[per((1, 1, 512)), per((1, 2, hd, m))],
        out_specs=per((1, 1, MEM_W)),
        out_shape=jax.ShapeDtypeStruct((s, 1, MEM_W), F32),
        compiler_params=_cp(("arbitrary",)),
        name="mem_decode",
    )(qm3, mem_t)


def _prep_w_in(w_in):
    d = w_in.shape[0]
    q_w, kv_w, g_w, zn_w, u_w, zr_w, qm_w, zm_w = jnp.split(
        w_in, [int(c) for c in np.cumsum([512, 768, 24, 512, 896, 256, 256])], axis=1)
    zero64 = jnp.zeros((d, HEAD_DIM), w_in.dtype)
    q_slots = []
    for hd in range(NSA_HEADS):
        wq = q_w[:, hd * HEAD_DIM:(hd + 1) * HEAD_DIM]
        q_slots += [wq, zero64] if hd // NSA_HPG == 0 else [zero64, wq]
    qm_slots = []
    for h in range(MEM_HEADS):
        wq = qm_w[:, h * HEAD_DIM:(h + 1) * HEAD_DIM]
        qm_slots += [wq, zero64] if h % 2 == 0 else [zero64, wq]
    g_pad = jnp.pad(g_w, ((0, 0), (0, LANES - g_w.shape[1])))
    w_p = jnp.concatenate(q_slots + [kv_w, zn_w, zr_w, zm_w, u_w] + qm_slots + [g_pad], axis=1)
    return w_p.astype(BF16)


def _rope_table(pos):
    half = HEAD_DIM // 2
    inv = ROPE_THETA ** (-2.0 * jnp.arange(half, dtype=F32) / HEAD_DIM)
    ang = pos.astype(F32)[:, None] * inv[None, :]
    cos = jnp.cos(ang)
    sin = jnp.sin(ang)
    return jnp.tile(jnp.concatenate([cos, cos], axis=1), (1, 2)), jnp.tile(jnp.concatenate([-sin, sin], axis=1), (1, 2))


def _block_diag(blocks):
    n = len(blocks)
    r, c = blocks[0].shape
    out = jnp.zeros((n * r, n * c), blocks[0].dtype)
    for i, blk in enumerate(blocks):
        out = out.at[i * r:(i + 1) * r, i * c:(i + 1) * c].set(blk)
    return out


def _prep_compress(cmp_pe, cmp_w1, cmp_b1, cmp_w2, cmp_b2, ck_norm, cend):
    eg = [(0, 0), (0, 1), (1, 0), (1, 1)]
    pelo = jnp.concatenate([cmp_pe[e, :CMP_STRIDE] for e, _ in eg], axis=1)
    pehi = jnp.concatenate([cmp_pe[e, CMP_STRIDE:] for e, _ in eg], axis=1)
    w1lo = jnp.stack([_block_diag([cmp_w1[e, p] for e, _ in eg]) for p in range(CMP_STRIDE)]).astype(BF16)
    w1hi = jnp.stack([_block_diag([cmp_w1[e, CMP_STRIDE + p] for e, _ in eg]) for p in range(CMP_STRIDE)]).astype(BF16)
    b1 = jnp.concatenate([cmp_b1[e] for e, _ in eg])[None, :]
    w2 = _block_diag([cmp_w2[e] for e, _ in eg]).astype(BF16)
    b2 = jnp.concatenate([cmp_b2[e] for e, _ in eg])[None, :]
    ckn = jnp.tile(ck_norm, 2)[None, :]
    cos_c, sin_c = _rope_table(cend)
    return [pelo, pehi, w1lo, w1hi, b1, w2, b2, ckn, cos_c, sin_c]


def _overlap(n_cb, n_sb):
    cstart = np.arange(n_cb)[:, None] * CMP_STRIDE
    sstart = np.arange(n_sb)[None, :] * SEL_BLK
    ov = np.clip(np.minimum(cstart + CMP_BLK, sstart + SEL_BLK) - np.maximum(cstart, sstart), 0, None)
    return ov.astype(np.float32) / CMP_BLK


def _pad_to(a, rows, cols):
    return np.pad(a, ((0, rows - a.shape[0]), (0, cols - a.shape[1])))


def kernel(x_prompt, x_sample, mem_prompt, cache_nsa, cache_win, cache_mem, state_rwkv_shift, state_rwkv_wkv,
           page_table, ln_g, w_in, nsa_q_norm, nsa_k_norm, cmp_pe, cmp_w1, cmp_b1, cmp_w2, cmp_b2, rwkv_mu, rwkv_w0,
           rwkv_w2, rwkv_a0, rwkv_a2, rwkv_k_k, rwkv_k_a, rwkv_r_k, rwkv_ln_w, rwkv_ln_b, mem_norm_g, w_mem_kv,
           mem_q_norm, mem_k_norm, w_out):
    depth = ln_g.shape[0]
    assert depth == 1
    bp, tp, d = x_prompt.shape
    bs, ts, _ = x_sample.shape
    assert ts == 1
    n_pages = page_table.shape[1]
    p_len = n_pages * PAGE_SIZE
    n_pool = cache_nsa.shape[1]
    wlen = cache_win.shape[2]
    mlen = cache_mem.shape[2]
    assert tp % KTILE == 0 and tp >= WINDOW + QBLK and n_pages % PAGES_PER_STEP == 0 and wlen == WINDOW
    l = 0

    bd = jnp.asarray(np.kron(np.eye(2), np.ones((HEAD_DIM, HEAD_DIM))), BF16)
    w_p = _prep_w_in(w_in[l])
    lng = ln_g[l][None, :]
    qn = jnp.tile(nsa_q_norm[l], 16)[None, :]
    kn = jnp.tile(nsa_k_norm[l], (1, 2))
    mqn = jnp.tile(mem_q_norm[l], 8)[None, :]
    w_out_b = w_out[l].astype(BF16)
    rw_vecs = [rwkv_mu[l], rwkv_w0[l], rwkv_a0[l], rwkv_k_k[l], rwkv_k_a[l], rwkv_r_k[l].reshape(RWKV_W),
               rwkv_ln_w[l], rwkv_ln_b[l]]
    mu, w0, a0, kkp, kap, rkp, lnw, lnb = rw_vecs
    rw_args = [mu[None, :], w0[None, :], rwkv_w2[l], a0[None, :], rwkv_a2[l], kkp[None, :], kap[None, :],
               rkp[None, :], lnw[None, :], lnb[None, :]]
    rw_cols = [mu[:, None], w0[:, None], rwkv_w2[l].T, a0[:, None], rwkv_a2[l].T, kkp[:, None], kap[:, None],
               rkp[:, None], lnw[:, None], lnb[:, None]]

    tb = 256
    xp2 = x_prompt.reshape(bp * tp, d)
    cos_p, sin_p = _rope_table(jnp.arange(tp))
    q, rows4, rows4t, rows_w, kvb, vt, zs, u_rw, qm, gates = _project(
        xp2, bp, tp, lng, w_p, cos_p, sin_p, tp // tb, qn, kn, mqn, bd, tb)

    n_ch = tp // CMP_STRIDE
    cend_p = jnp.arange(n_ch) * CMP_STRIDE + CMP_BLK - 1
    cmp_args_p = _prep_compress(cmp_pe[l], cmp_w1[l], cmp_b1[l], cmp_w2[l], cmp_b2[l], nsa_k_norm[l, 0], cend_p) + [bd]
    ck, cv = _compress_prompt(rows4, cmp_args_p, bp, tp)
    n_sb = tp // SEL_BLK
    n_sbp = -(-n_sb // LANES) * LANES
    ov = _overlap(n_ch - 1, n_sb)
    ovt = jnp.asarray(_pad_to(ov.T, n_sbp, n_ch), BF16)
    o_nsa = _nsa_prompt(q, gates, kvb, vt, ck, cv, ovt, bp, tp)

    nbat = max(n for n in (4, 2, 1) if bp % n == 0)
    o_rw, wkv_p = _rwkv_prompt(u_rw.reshape(bp, tp, RWKV_SHIFT_W), rw_args, bd, nbat)
    o_rw = o_rw.reshape(bp * tp, RWKV_W)

    kmn = jnp.tile(mem_k_norm[l], 4)[None, :]
    memkv_p = _mem_kv(mem_prompt, mem_norm_g[l][None, :], w_mem_kv[l].astype(BF16), kmn, bd)
    o_mem = _mem_attn(qm, memkv_p, bp, tp, 512)
    y_prompt = _out_proj(xp2, o_nsa, o_rw, o_mem, zs, w_out_b, 256).reshape(bp, tp, d)

    nsa_rows_prompt = rows4t.reshape(bp, 4, NSA_KV_HEADS, HEAD_DIM, tp).transpose(0, 4, 1, 2, 3)[None]
    win_p = min(WINDOW, tp)
    win_prompt = rows_w.reshape(bp, tp, 2, NSA_KV_HEADS, HEAD_DIM)[None, :, tp - win_p:]
    shift_prompt = u_rw.reshape(bp, tp, RWKV_SHIFT_W)[None, :, -1]
    mem_kv_prompt = memkv_p.reshape(1, bp, mlen, 2, MEM_HEADS, HEAD_DIM)

    xs2 = x_sample.reshape(bs, d)
    cos_s, sin_s = _rope_table(jnp.full((bs,), p_len))
    q_s, rows4_s, rows4t_s, rows_w_s, _, _, zs_s, u_s, qm_s, gates_s = _project(
        xs2, 1, bs, lng, w_p, cos_s, sin_s, 1, qn, kn, mqn, bd, bs)

    n_ch_s = p_len // CMP_STRIDE
    cend_s = jnp.arange(n_ch_s) * CMP_STRIDE + CMP_BLK - 1
    cmp_args_s = _prep_compress(cmp_pe[l], cmp_w1[l], cmp_b1[l], cmp_w2[l], cmp_b2[l], nsa_k_norm[l, 0], cend_s) + [bd]
    n_sb_s = p_len // SEL_BLK + 1
    n_sbp_s = -(-n_sb_s // LANES) * LANES
    ov_s = jnp.asarray(_pad_to(_overlap(n_ch_s - 1, n_sb_s), n_ch_s, n_sbp_s), BF16)
    pool_t = cache_nsa[l].transpose(0, 2, 3, 4, 1).reshape(n_pool, 4, NSA_KV_HEADS * HEAD_DIM, PAGE_SIZE)
    pe_t = cmp_pe[l].reshape(2, 2, CMP_STRIDE, HEAD_DIM).transpose(1, 0, 3, 2)
    pet = jnp.tile(pe_t, (1, 1, NSA_KV_HEADS, PAGE_SIZE // CMP_STRIDE))
    ri = np.arange(PAGE_SIZE)
    perm = jnp.asarray(ri[None, :] == (CMP_STRIDE * (ri % (PAGE_SIZE // CMP_STRIDE)) + ri // (PAGE_SIZE // CMP_STRIDE))[:, None], BF16)
    oc, idx = _decode_a(page_table, pool_t, q_s.reshape(bs, 1, 1024), pet, perm, cmp_args_s, ov_s, bs, p_len)
    idx32 = idx[:, 0:NSA_KV_HEADS, 0:N_SEL].reshape(bs, NSA_KV_HEADS * N_SEL)
    win_t = cache_win[l].transpose(0, 2, 3, 4, 1).reshape(bs, 2, NSA_KV_HEADS * HEAD_DIM, wlen)
    o_nsa_s, win_s = _decode_b(page_table, idx32, pool_t, q_s.reshape(bs, 1, 1024), gates_s.reshape(bs, 1, LANES), oc,
                               rows4_s.reshape(bs, 1, 512), rows_w_s.reshape(bs, 1, 256), win_t, bs, p_len)

    state_t = state_rwkv_wkv[l].transpose(1, 2, 3, 0)
    o_rw_s, wkv_s_t = _rwkv_decode(u_s, state_rwkv_shift[l], state_t, rw_cols)

    mem_t = cache_mem[l].transpose(0, 2, 3, 4, 1).reshape(bs, 2, MEM_HEADS * HEAD_DIM, mlen)
    o_mem_s = _mem_decode(qm_s.reshape(bs, 1, 512), mem_t).reshape(bs, MEM_W)
    y_sample = _out_proj(xs2, o_nsa_s.reshape(bs, 512), o_rw_s, o_mem_s, zs_s, w_out_b, bs).reshape(bs, 1, d)

    nsa_rows_sample = rows4t_s.reshape(4, NSA_KV_HEADS, HEAD_DIM, bs).transpose(3, 0, 1, 2)[None, :, None]
    win_sample = win_s.reshape(bs, 2, NSA_KV_HEADS, HEAD_DIM, wlen).transpose(0, 4, 1, 2, 3)[None]
    shift_sample = u_s[None]
    wkv_sample = wkv_s_t.transpose(3, 0, 1, 2)[None]
    return (y_prompt, y_sample, nsa_rows_prompt, win_prompt, shift_prompt, wkv_p[None], mem_kv_prompt,
            nsa_rows_sample, win_sample, shift_sample, wkv_sample)
```

```python
import functools

import numpy as np
import jax
import jax.numpy as jnp
from jax import lax
from jax.experimental import pallas as pl
from jax.experimental.pallas import tpu as pltpu

F32 = jnp.float32
BF16 = jnp.bfloat16

HEAD_DIM = 64
NSA_HEADS = 8
NSA_KV_HEADS = 2
NSA_HPG = 4
RWKV_HEADS = 4
MEM_HEADS = 4
NSA_W = 512
RWKV_W = 256
MEM_W = 256
CMP_BLK = 32
CMP_STRIDE = 16
SEL_BLK = 64
N_SEL = 16
WINDOW = 512
PAGE_SIZE = 128
RWKV_SHIFT_W = 896
ROPE_THETA = 10000.0
RMS_EPS = 1e-6
GN_EPS = 64e-5
SCALE = HEAD_DIM ** -0.5
LOG2E = 1.4426950408889634

LANES = 128
QBLK = 128
KTILE = 1024
CHUNK = 128
RWKV_CHUNK = 64
NEG = -1e30

C_Q = 0
C_KV = 1024
C_Z = 1792
C_U = 2816
C_QM = 3712
C_G = 4224
N_INP = 4352

VMEM_LIMIT = 56 * 1024 * 1024


def _cp(sem, flags=None):
    return pltpu.CompilerParams(dimension_semantics=sem, vmem_limit_bytes=VMEM_LIMIT, flags=flags)


def _nt(a, b):
    return lax.dot_general(a, b, (((1,), (1,)), ((), ())), preferred_element_type=F32)


def _nn(a, b):
    return jnp.dot(a, b, preferred_element_type=F32)


def _nnb(a, b):
    return _nn(a.astype(BF16), b.astype(BF16))


def _ntb(a, b):
    return _nt(a.astype(BF16), b.astype(BF16))


def _split2(x):
    hi = x.astype(BF16)
    lo = (x - hi.astype(F32)).astype(BF16)
    return hi, lo


def _split3(x):
    hi = x.astype(BF16)
    r = x - hi.astype(F32)
    mid = r.astype(BF16)
    lo = (r - mid.astype(F32)).astype(BF16)
    return hi, mid, lo


def _nn_x(a, b):
    ah, al = _split2(a)
    bh, bl = _split2(b)
    return _nn(ah, bh) + _nn(ah, bl) + _nn(al, bh)


def _segsum(x, bd):
    cols = []
    for c in range(x.shape[1] // LANES):
        hi, lo = _split2(x[:, c * LANES:(c + 1) * LANES])
        cols.append(_nn(hi, bd) + _nn(lo, bd))
    return cols[0] if len(cols) == 1 else jnp.concatenate(cols, axis=1)


def _rot_half(x):
    lane = lax.broadcasted_iota(jnp.int32, x.shape, 1)
    up = pltpu.roll(x, 96, axis=1)
    dn = pltpu.roll(x, 32, axis=1)
    return jnp.where((lane & 63) < 32, up, dn)


def _rope_cols(v, cosf, sinf):
    cols = []
    for c in range(v.shape[1] // LANES):
        xc = v[:, c * LANES:(c + 1) * LANES]
        cols.append(xc * cosf + _rot_half(xc) * sinf)
    return cols[0] if len(cols) == 1 else jnp.concatenate(cols, axis=1)


def _head_rms(v, gain, bd):
    ss = _segsum(v * v, bd)
    return v * lax.rsqrt(ss * (1.0 / HEAD_DIM) + RMS_EPS) * gain


def _sigmoid(x):
    return 1.0 / (1.0 + jnp.exp(-x))


def _eye(n, dtype=BF16):
    r0 = lax.broadcasted_iota(jnp.int32, (n, n), 0)
    r1 = lax.broadcasted_iota(jnp.int32, (n, n), 1)
    return jnp.where(r0 == r1, 1.0, 0.0).astype(dtype)


def _softmax2_cols(s, bias, col_ok=None):
    ps = []
    for h in range(s.shape[1] // LANES):
        sm = s[:, h * LANES:(h + 1) * LANES] + bias
        m = jnp.max(sm, axis=0, keepdims=True)
        p = jnp.exp2(sm - m)
        r = 1.0 / jnp.sum(p, axis=0, keepdims=True)
        if col_ok is not None:
            r = jnp.where(col_ok, r, 0.0)
        ps.append(p * r)
    return jnp.concatenate(ps, axis=1)


def _exp2_cols_bf16(s, bias):
    ps = []
    for h in range(s.shape[1] // LANES):
        sm = s[:, h * LANES:(h + 1) * LANES] + bias
        ps.append(jnp.exp2((sm - jnp.max(sm, axis=0, keepdims=True)).astype(BF16)))
    return jnp.concatenate(ps, axis=1)


def _with_ones(vt):
    return jnp.concatenate([vt, jnp.ones((16, vt.shape[1]), BF16)], axis=0)


def _full(shape):
    nd = len(shape)
    return pl.BlockSpec(shape, lambda *_: (0,) * nd)


def _proj_body(x_ref, lng_ref, w_ref, cos_ref, sin_ref, qn_ref, kn_ref, mqn_ref, bd_ref,
               q_out, rows4_out, rows4t_out, rowsw_out, kvb_out, vt_out, zs_out, u_out, qm_out, g_out):
    x = x_ref[...]
    ms = jnp.mean(x * x, axis=-1, keepdims=True)
    xn = (x * lax.rsqrt(ms + RMS_EPS) * lng_ref[...]).astype(BF16)
    bd = bd_ref[...]
    cosf = cos_ref[...]
    sinf = sin_ref[...]

    def seg(a, b):
        return jnp.dot(xn, w_ref[:, a:b], preferred_element_type=F32)

    q = _rope_cols(_head_rms(seg(C_Q, C_KV), qn_ref[...], bd), cosf, sinf) * (SCALE * LOG2E)
    q_out[...] = q.astype(BF16)

    kv = seg(C_KV, C_Z)
    kn = kn_ref[...]
    ksel = _rope_cols(_head_rms(kv[:, 256:384], kn[1:2], bd), cosf, sinf)
    kwin = _rope_cols(_head_rms(kv[:, 512:640], kn[2:3], bd), cosf, sinf)
    rows4 = jnp.concatenate([kv[:, 0:256], ksel, kv[:, 384:512]], axis=1)
    rows4_out[...] = rows4
    rows4t_out[0] = rows4.T
    rowsw_out[:, 0:128] = kwin
    rowsw_out[:, 128:256] = kv[:, 640:768]
    kvb_out[:, 0:128] = ksel.astype(BF16)
    kvb_out[:, 128:256] = kv[:, 384:512].astype(BF16)
    kvb_out[:, 256:384] = kwin.astype(BF16)
    kvb_out[:, 384:512] = kv[:, 640:768].astype(BF16)
    vt = jnp.concatenate([kv[:, 384:512], kv[:, 640:768]], axis=1).T
    vt_out[0, 0] = vt[0:128].astype(BF16)
    vt_out[0, 1] = vt[128:256].astype(BF16)

    z = seg(C_Z, C_U)
    zs_out[...] = z * _sigmoid(z)
    u_out[...] = seg(C_U, C_QM)
    qm = _head_rms(seg(C_QM, C_G), mqn_ref[...], bd) * (SCALE * LOG2E)
    qm_out[...] = qm.astype(BF16)
    g_out[...] = _sigmoid(seg(C_G, N_INP))


def _project(x2d, nb, t, lng, w_p, cos_t, sin_t, n_tab_blocks, qn, kn, mqn, bd, tb):
    n, d = x2d.shape
    nblk = t // tb
    row = lambda i: (i, 0)
    tab = lambda i: (i % n_tab_blocks, 0)
    outs = [
        jax.ShapeDtypeStruct((n, 1024), BF16),
        jax.ShapeDtypeStruct((n, 512), F32),
        jax.ShapeDtypeStruct((nb, 512, t), F32),
        jax.ShapeDtypeStruct((n, 256), F32),
        jax.ShapeDtypeStruct((n, 512), BF16),
        jax.ShapeDtypeStruct((nb, 2, LANES, t), BF16),
        jax.ShapeDtypeStruct((n, 1024), F32),
        jax.ShapeDtypeStruct((n, RWKV_SHIFT_W), F32),
        jax.ShapeDtypeStruct((n, 512), BF16),
        jax.ShapeDtypeStruct((n, 128), F32),
    ]
    out_specs = [pl.BlockSpec((tb, o.shape[1]), row) for o in outs]
    out_specs[2] = pl.BlockSpec((1, 512, tb), lambda i: (i // nblk, 0, i % nblk))
    out_specs[5] = pl.BlockSpec((1, 2, LANES, tb), lambda i: (i // nblk, 0, 0, i % nblk))
    return pl.pallas_call(
        _proj_body,
        grid=(n // tb,),
        in_specs=[
            pl.BlockSpec((tb, d), row),
            _full((1, d)),
            _full((d, N_INP)),
            pl.BlockSpec((tb, LANES), tab),
            pl.BlockSpec((tb, LANES), tab),
            _full((1, 1024)),
            _full((3, LANES)),
            _full((1, 512)),
            _full((LANES, LANES)),
        ],
        out_specs=out_specs,
        out_shape=outs,
        compiler_params=_cp(("arbitrary",)),
        name="proj",
    )(x2d, lng, w_p, cos_t, sin_t, qn, kn, mqn, bd)


def _compress_tail(lo, hi, n_ch, b1_ref, w2_ref, b2_ref, ckn_ref, cos_ref, sin_ref, bd_ref):
    his = pltpu.roll(hi, n_ch - 1, axis=0)
    pre = lo + his + b1_ref[...]
    hid = pre * _sigmoid(pre)
    out2 = _nn(hid.astype(BF16), w2_ref[...]) + b2_ref[...]
    ck = _head_rms(out2[:, 0:128], ckn_ref[...], bd_ref[...])
    ck = _rope_cols(ck, cos_ref[...], sin_ref[...])
    return ck, out2[:, 128:256]


def _compress_compute(xk_ref, xv_ref, n_ch, pelo_ref, pehi_ref, w1lo_ref, w1hi_ref, b1_ref, w2_ref, b2_ref,
                      ckn_ref, cos_ref, sin_ref, bd_ref):
    lo = jnp.zeros((n_ch, 256), F32)
    hi = jnp.zeros((n_ch, 256), F32)
    for p in range(CMP_STRIDE):
        xp = jnp.concatenate([xk_ref[pl.ds(p, n_ch, stride=CMP_STRIDE), :],
                              xv_ref[pl.ds(p, n_ch, stride=CMP_STRIDE), :]], axis=1)
        lo = lo + _nn((xp + pelo_ref[p:p + 1, :]).astype(BF16), w1lo_ref[p])
        hi = hi + _nn((xp + pehi_ref[p:p + 1, :]).astype(BF16), w1hi_ref[p])
    return _compress_tail(lo, hi, n_ch, b1_ref, w2_ref, b2_ref, ckn_ref, cos_ref, sin_ref, bd_ref)


def _compress_body(xk_ref, xv_ref, pelo_ref, pehi_ref, w1lo_ref, w1hi_ref, b1_ref, w2_ref, b2_ref,
                   ckn_ref, cos_ref, sin_ref, bd_ref, ck_out, cv_out, *, n_ch):
    ck, cv = _compress_compute(xk_ref, xv_ref, n_ch, pelo_ref, pehi_ref, w1lo_ref, w1hi_ref, b1_ref, w2_ref,
                               b2_ref, ckn_ref, cos_ref, sin_ref, bd_ref)
    ck_out[0] = ck.astype(BF16)
    cv_out[0] = cv.T.astype(BF16)


def _cmp_specs(n_ch):
    return [
        _full((CMP_STRIDE, 256)), _full((CMP_STRIDE, 256)),
        _full((CMP_STRIDE, 256, 256)), _full((CMP_STRIDE, 256, 256)),
        _full((1, 256)), _full((256, 256)), _full((1, 256)),
        _full((1, LANES)), _full((n_ch, LANES)), _full((n_ch, LANES)), _full((LANES, LANES)),
    ]


def _compress_prompt(rows4, cmp_args, b, t):
    n_ch = t // CMP_STRIDE
    outs = [jax.ShapeDtypeStruct((b, n_ch, LANES), BF16), jax.ShapeDtypeStruct((b, LANES, n_ch), BF16)]
    return pl.pallas_call(
        functools.partial(_compress_body, n_ch=n_ch),
        grid=(b,),
        in_specs=[pl.BlockSpec((t, LANES), lambda i: (i, 0)), pl.BlockSpec((t, LANES), lambda i: (i, 1))]
        + _cmp_specs(n_ch),
        out_specs=[pl.BlockSpec((1, n_ch, LANES), lambda i: (i, 0, 0)), pl.BlockSpec((1, LANES, n_ch), lambda i: (i, 0, 0))],
        out_shape=outs,
        compiler_params=_cp(("arbitrary",)),
        name="compress_prompt",
    )(rows4, rows4, *cmp_args)


def _select_blocks(score, n_sel):
    jj = lax.broadcasted_iota(jnp.int32, score.shape, 0)
    big = jnp.int32(score.shape[0])
    sel = jnp.zeros(score.shape, F32)
    for _ in range(n_sel):
        mx = jnp.max(score, axis=0, keepdims=True)
        first = jnp.min(jnp.where(score == mx, jj, big), axis=0, keepdims=True)
        pick = jj == first
        sel = jnp.where(pick, 1.0, sel)
        score = jnp.where(pick, -jnp.inf, score)
    return sel


def _nsa_prompt_body(q_ref, g_ref, kvb_ref, vt_ref, ck_ref, cvt_ref, ovt_ref, o_ref, sel_sc, *, n_cb, n_sb):
    i = pl.program_id(1)
    start = i * QBLK
    nh = NSA_HEADS
    q = q_ref[...]
    qall = jnp.concatenate([q[:, hd * LANES:(hd + 1) * LANES] for hd in range(nh)], axis=0)
    gT = g_ref[...].T
    qpos = start + lax.broadcasted_iota(jnp.int32, (1, QBLK), 1)
    qpos2 = jnp.concatenate([qpos, qpos], axis=1)
    wk = WINDOW + QBLK
    n_tiles = (start + QBLK + KTILE - 1) // KTILE
    lo_w = pl.multiple_of(jnp.maximum(start - WINDOW, 0), QBLK)

    cend = lax.broadcasted_iota(jnp.int32, (n_cb, 1), 0) * CMP_STRIDE + (CMP_BLK - 1)
    bias_c = jnp.where(cend <= qpos, 0.0, NEG)
    p = _softmax2_cols(_nt(ck_ref[0], qall), bias_c, qpos >= CMP_BLK - 1)
    cvt = cvt_ref[0]
    ocT, psums = [], []
    for g in range(NSA_KV_HEADS):
        pg = p[:, g * 512:(g + 1) * 512]
        ocT.append(_nn(cvt[g * 64:(g + 1) * 64], pg.astype(BF16)))
        psums.append(pg[:, 0:128] + pg[:, 128:256] + pg[:, 256:384] + pg[:, 384:512])
    ph, plo = _split2(jnp.concatenate(psums, axis=1))
    imp = _nn(ovt_ref[...], ph) + _nn(ovt_ref[...], plo)

    jj = lax.broadcasted_iota(jnp.int32, (n_sb, 2 * QBLK), 0)
    cur = lax.shift_right_logical(qpos2, 6)
    forced = (jj == 0) | (jj == cur) | (jj == cur - 1)
    score = jnp.where(jj * SEL_BLK <= qpos2, jnp.where(forced, jnp.inf, imp), -jnp.inf)
    sel_sc[...] = _select_blocks(score, N_SEL)

    bpt = KTILE // SEL_BLK

    def tile_step(t, carry):
        m, acc0, acc1 = carry
        k0 = pl.multiple_of(t * KTILE, KTILE)
        s = _nt(kvb_ref[pl.ds(k0, KTILE), 0:128], qall)
        sel8 = sel_sc[pl.ds(pl.multiple_of(t * bpt, bpt), bpt), :]
        member = jnp.concatenate([jnp.broadcast_to(sel8[b:b + 1, :], (SEL_BLK, 2 * QBLK)) for b in range(bpt)], axis=0)
        kpos = k0 + lax.broadcasted_iota(jnp.int32, (KTILE, 1), 0)
        bias = jnp.where((member > 0.5) & (kpos <= qpos2), 0.0, NEG)
        vT = vt_ref[0, 0, :, pl.ds(k0, KTILE)]
        ms, als, ps = [], [], []
        for hd in range(nh):
            g = hd // NSA_HPG
            cs = slice(hd * LANES, (hd + 1) * LANES)
            sm = s[:, cs] + bias[:, g * LANES:(g + 1) * LANES]
            mn = jnp.maximum(m[:, cs], jnp.max(sm, axis=0, keepdims=True))
            ms.append(mn)
            als.append(jnp.exp2(m[:, cs] - mn))
            ps.append(jnp.exp2((sm - mn).astype(BF16)))
        acc0 = jnp.concatenate(als[0:4], axis=1) * acc0 + _nn(_with_ones(vT[0:64]), jnp.concatenate(ps[0:4], axis=1))
        acc1 = jnp.concatenate(als[4:8], axis=1) * acc1 + _nn(_with_ones(vT[64:128]), jnp.concatenate(ps[4:8], axis=1))
        return jnp.concatenate(ms, axis=1), acc0, acc1

    m0 = jnp.full((1, nh * QBLK), NEG, F32)
    a0 = jnp.zeros((HEAD_DIM + 16, 4 * QBLK), F32)
    _, acc0, acc1 = lax.fori_loop(0, n_tiles, tile_step, (m0, a0, a0))
    osT = [a[0:HEAD_DIM] * (1.0 / a[HEAD_DIM:HEAD_DIM + 1]) for a in (acc0, acc1)]

    kpos_w = lo_w + lax.broadcasted_iota(jnp.int32, (wk, 1), 0)
    bias_w = jnp.where((kpos_w <= qpos) & (kpos_w > qpos - WINDOW), 0.0, NEG)
    pw = _exp2_cols_bf16(_nt(kvb_ref[pl.ds(lo_w, wk), 256:384], qall), bias_w)
    vwT = vt_ref[0, 1, :, pl.ds(lo_w, wk)]
    owT = []
    for g in range(NSA_KV_HEADS):
        a = _nn(_with_ones(vwT[g * 64:(g + 1) * 64]), pw[:, g * 512:(g + 1) * 512])
        owT.append(a[0:HEAD_DIM] * (1.0 / a[HEAD_DIM:HEAD_DIM + 1]))

    cols = []
    for c in range(4):
        g = c // 2
        parts = []
        for hd in (2 * c, 2 * c + 1):
            cs = slice((hd % 4) * QBLK, (hd % 4 + 1) * QBLK)
            parts.append(gT[3 * hd:3 * hd + 1, :] * ocT[g][:, cs] + gT[3 * hd + 1:3 * hd + 2, :] * osT[g][:, cs]
                         + gT[3 * hd + 2:3 * hd + 3, :] * owT[g][:, cs])
        cols.append(jnp.concatenate(parts, axis=0).T)
    o_ref[...] = jnp.concatenate(cols, axis=1)


def _nsa_prompt(q, gates, kvb, vt, ck, cvt, ovt, b, t):
    nq = t // QBLK
    n_cb = ck.shape[1]
    n_sb = ovt.shape[0]
    return pl.pallas_call(
        functools.partial(_nsa_prompt_body, n_cb=n_cb, n_sb=n_sb),
        grid=(b, nq),
        in_specs=[
            pl.BlockSpec((QBLK, 1024), lambda bi, i: (bi * nq + i, 0)),
            pl.BlockSpec((QBLK, LANES), lambda bi, i: (bi * nq + i, 0)),
            pl.BlockSpec((t, 512), lambda bi, i: (bi, 0)),
            pl.BlockSpec((1, 2, LANES, t), lambda bi, i: (bi, 0, 0, 0)),
            pl.BlockSpec((1, n_cb, LANES), lambda bi, i: (bi, 0, 0)),
            pl.BlockSpec((1, LANES, n_cb), lambda bi, i: (bi, 0, 0)),
            _full(ovt.shape),
        ],
        out_specs=pl.BlockSpec((QBLK, 512), lambda bi, i: (bi * nq + i, 0)),
        out_shape=jax.ShapeDtypeStruct((b * t, 512), F32),
        scratch_shapes=[pltpu.VMEM((n_sb, 2 * QBLK), F32)],
        compiler_params=_cp(("arbitrary", "arbitrary")),
        name="nsa_prompt",
    )(q, gates, kvb, vt, ck, cvt, ovt)


def _rwkv_prep(u, mu, w0, w2, a0, a2, kkp, kap, bd):
    r = u[:, 0:256]
    k = u[:, 256:512]
    v = u[:, 512:768]
    ww = w0 + _nn_x(jnp.tanh(u[:, 768:832]), w2)
    nw = -ww
    sp = jnp.maximum(nw, 0.0) + jnp.log(1.0 + jnp.exp(-jnp.abs(nw)))
    logdec = -jnp.exp(-sp - 0.5)
    a = _sigmoid(a0 + _nn_x(u[:, 832:896], a2))
    kk = k * kkp
    kk = kk / jnp.maximum(jnp.sqrt(_segsum(kk * kk, bd)), 1e-12)
    k2 = k * (1.0 + (a - 1.0) * kap)
    return r, k2, v, kk, kk * a, logdec


def _bmm_nt(a, b):
    return lax.dot_general(a, b, (((2,), (2,)), ((0,), (0,))), preferred_element_type=F32)


def _bmm(a, b):
    return lax.dot_general(a, b, (((2,), (1,)), ((0,), (0,))), preferred_element_type=F32)


def _rwkv_body(u_ref, mu_ref, w0_ref, w2_ref, a0_ref, a2_ref, kk_ref, ka_ref, rk_ref,
               lnw_ref, lnb_ref, bd_ref, o_ref, sout_ref, s_sc, prev_sc, *, nbat, n_dbl):
    c = pl.program_id(1)
    C = RWKV_CHUNK
    D = HEAD_DIM
    R = nbat * C
    G = nbat * RWKV_HEADS

    @pl.when(c == 0)
    def _():
        s_sc[...] = jnp.zeros(s_sc.shape, F32)
        prev_sc[...] = jnp.zeros(prev_sc.shape, F32)

    bd = bd_ref[...]
    rowi = lax.broadcasted_iota(jnp.int32, (C, 1), 0)
    us = []
    for n in range(nbat):
        ucur = u_ref[n]
        prev = jnp.where(rowi == 0, prev_sc[n], pltpu.roll(ucur, 1, axis=0))
        prev_sc[n] = ucur[C - 1:C, :]
        us.append(ucur + (prev - ucur) * mu_ref[...])
    u = us[0] if nbat == 1 else jnp.concatenate(us, axis=0)
    r, k2, v, kk, bb, logdec = _rwkv_prep(u, mu_ref[...], w0_ref[...], w2_ref[...], a0_ref[...], a2_ref[...],
                                          kk_ref[...], ka_ref[...], bd)

    ri = lax.broadcasted_iota(jnp.int32, (R, R), 0)
    ci = lax.broadcasted_iota(jnp.int32, (R, R), 1)
    same = lax.shift_right_logical(ri, 6) == lax.shift_right_logical(ci, 6)
    tri = jnp.where(same & (ri >= ci), 1.0, 0.0).astype(BF16)
    ones = jnp.where(same, 1.0, 0.0).astype(BF16)
    d1, d2, d3 = _split3(logdec)
    lcum = _nn(tri, d1) + _nn(tri, d2) + _nn(tri, d3)
    lend = _nn(ones, d1) + _nn(ones, d2) + _nn(ones, d3)
    e_neg = jnp.exp(-lcum)
    e_end = jnp.exp(lend - lcum)
    gend = jnp.exp(lend)

    def grp(x):
        return jnp.stack([x[n * C:(n + 1) * C, h * D:(h + 1) * D] for n in range(nbat) for h in range(RWKV_HEADS)])

    ar = jnp.concatenate([grp(-kk * jnp.exp(lcum - logdec)), grp(r * jnp.exp(lcum))], axis=1).astype(BF16)
    bk = jnp.concatenate([grp(bb * e_neg), grp(k2 * e_neg)], axis=1).astype(BF16)
    bkg = jnp.concatenate([grp(bb * e_end), grp(k2 * e_end)], axis=1).astype(BF16)
    vg = grp(v).astype(BF16)
    gg = jnp.stack([gend[n * C:n * C + 1, h * D:(h + 1) * D] for n in range(nbat) for h in range(RWKV_HEADS)])

    ti = lax.broadcasted_iota(jnp.int32, (1, C, C), 1)
    ii = lax.broadcasted_iota(jnp.int32, (1, C, C), 2)
    strict = ti > ii
    incl = ti >= ii
    s_old = s_sc[...]
    mm = _bmm_nt(ar, bk)
    lab = jnp.where(strict, mm[:, 0:C, 0:C], 0.0)
    lak = jnp.where(strict, mm[:, 0:C, C:2 * C], 0.0)
    mrb = jnp.where(incl, mm[:, C:2 * C, 0:C], 0.0)
    mrk = jnp.where(incl, mm[:, C:2 * C, C:2 * C], 0.0)
    ars = _bmm_nt(ar, s_old.astype(BF16))
    rhs = ars[:, 0:C] + _bmm(lak.astype(BF16), vg)
    x = jnp.where(ti == ii, 1.0, 0.0) + lab
    pw = lab
    for _ in range(n_dbl):
        pwb = pw.astype(BF16)
        pw = _bmm(pwb, pwb)
        x = x + _bmm(pw.astype(BF16), x.astype(BF16))
    uu = _bmm(x.astype(BF16), rhs.astype(BF16))
    yg = ars[:, C:2 * C] + _bmm(mrb.astype(BF16), uu.astype(BF16)) + _bmm(mrk.astype(BF16), vg)
    uv = jnp.concatenate([uu.astype(BF16), vg], axis=1)
    eye_g = jnp.broadcast_to(_eye(D)[None], (G, D, D))
    uvt = _bmm_nt(eye_g, uv).astype(BF16)
    s_sc[...] = s_old * gg + _bmm(uvt, bkg)

    y = jnp.concatenate([jnp.concatenate([yg[n * RWKV_HEADS + h] for h in range(RWKV_HEADS)], axis=1)
                         for n in range(nbat)], axis=0)
    mean = _segsum(y, bd) * (1.0 / D)
    dy = y - mean
    var = _segsum(dy * dy, bd) * (1.0 / D)
    yn = dy * lax.rsqrt(var + GN_EPS) * lnw_ref[...] + lnb_ref[...]
    out = yn + _segsum(r * k2 * rk_ref[...], bd) * v
    for n in range(nbat):
        o_ref[n] = out[n * C:(n + 1) * C]

    @pl.when(c == pl.num_programs(1) - 1)
    def _():
        for n in range(nbat):
            sout_ref[n] = s_sc[n * RWKV_HEADS:(n + 1) * RWKV_HEADS]


def _rwkv_prompt(u3, rw_args, bd, nbat):
    b, t, _ = u3.shape
    C = RWKV_CHUNK
    n_dbl = int(np.log2(C)) - 1
    outs = [jax.ShapeDtypeStruct((b, t, RWKV_W), F32),
            jax.ShapeDtypeStruct((b, RWKV_HEADS, HEAD_DIM, HEAD_DIM), F32)]
    vec = _full((1, RWKV_W))
    return pl.pallas_call(
        functools.partial(_rwkv_body, nbat=nbat, n_dbl=n_dbl),
        grid=(b // nbat, t // C),
        in_specs=[
            pl.BlockSpec((nbat, C, RWKV_SHIFT_W), lambda bi, c: (bi, c, 0)),
            _full((1, RWKV_SHIFT_W)),
            vec, _full((HEAD_DIM, RWKV_W)), vec, _full((HEAD_DIM, RWKV_W)),
            vec, vec, vec, vec, vec, _full((LANES, LANES)),
        ],
        out_specs=[pl.BlockSpec((nbat, C, RWKV_W), lambda bi, c: (bi, c, 0)),
                   pl.BlockSpec((nbat, RWKV_HEADS, HEAD_DIM, HEAD_DIM), lambda bi, c: (bi, 0, 0, 0))],
        out_shape=outs,
        scratch_shapes=[pltpu.VMEM((nbat * RWKV_HEADS, HEAD_DIM, HEAD_DIM), F32),
                        pltpu.VMEM((nbat, 1, RWKV_SHIFT_W), F32)],
        compiler_params=_cp(("arbitrary", "arbitrary")),
        name="rwkv_prompt",
    )(u3, *rw_args, bd)


def _rwkv_dec_body(u_ref, sh_ref, st_ref, mu_ref, w0_ref, w2t_ref, a0_ref, a2t_ref, kk_ref, ka_ref, rk_ref,
                   lnw_ref, lnb_ref, o_ref, so_ref, a_sc, w_sc, b_sc, k_sc, r_sc, v_sc, y_sc):
    h = pl.program_id(0)
    nh = pl.num_programs(0)
    D = HEAD_DIM

    @pl.when(h == 0)
    def _():
        uT = u_ref[...].T
        pT = sh_ref[...].T
        xT = uT + (pT - uT) * mu_ref[...]
        rT = xT[0:256]
        kT = xT[256:512]
        ww = w0_ref[...] + _nn_x(w2t_ref[...], jnp.tanh(xT[768:832]))
        nw = -ww
        sp = jnp.maximum(nw, 0.0) + jnp.log(1.0 + jnp.exp(-jnp.abs(nw)))
        w_sc[...] = jnp.exp(-jnp.exp(-sp - 0.5))
        aT = _sigmoid(a0_ref[...] + _nn_x(a2t_ref[...], xT[832:896]))
        kk = kT * kk_ref[...]
        for hh in range(RWKV_HEADS):
            blk = kk[hh * D:(hh + 1) * D]
            nrm = jnp.maximum(jnp.sqrt(jnp.sum(blk * blk, axis=0, keepdims=True)), 1e-12)
            a_sc[hh * D:(hh + 1) * D, :] = -(blk / nrm)
        b_sc[...] = -a_sc[...] * aT
        k_sc[...] = kT * (1.0 + (aT - 1.0) * ka_ref[...])
        r_sc[...] = rT
        v_sc[...] = xT[512:768]

    base = pl.multiple_of(h * D, D)
    at = a_sc[pl.ds(base, D), :]
    wt = w_sc[pl.ds(base, D), :]
    bt = b_sc[pl.ds(base, D), :]
    kt = k_sc[pl.ds(base, D), :]
    rt = r_sc[pl.ds(base, D), :]

    def vstep(vi, carry):
        s_v = st_ref[0, vi]
        sa = jnp.sum(s_v * at, axis=0, keepdims=True)
        vv = v_sc[pl.ds(base + vi, 1), :]
        s_n = s_v * wt + sa * bt + vv * kt
        so_ref[0, vi] = s_n
        y_sc[pl.ds(base + vi, 1), :] = jnp.sum(s_n * rt, axis=0, keepdims=True)
        return carry

    lax.fori_loop(0, D, vstep, 0)

    @pl.when(h == nh - 1)
    def _():
        outs = []
        for hh in range(RWKV_HEADS):
            sl = slice(hh * D, (hh + 1) * D)
            y = y_sc[sl, :]
            mean = jnp.mean(y, axis=0, keepdims=True)
            dy = y - mean
            var = jnp.mean(dy * dy, axis=0, keepdims=True)
            yn = dy * lax.rsqrt(var + GN_EPS) * lnw_ref[sl, :] + lnb_ref[sl, :]
            bonus = jnp.sum(r_sc[sl, :] * k_sc[sl, :] * rk_ref[sl, :], axis=0, keepdims=True) * v_sc[sl, :]
            outs.append(yn + bonus)
        o_ref[...] = jnp.concatenate(outs, axis=0).T


def _rwkv_decode(u_s, shift, state_t, rw_cols):
    s = u_s.shape[0]
    D = HEAD_DIM
    col = lambda n: _full((n, 1))
    vec_sc = pltpu.VMEM((RWKV_W, s), F32)
    return pl.pallas_call(
        _rwkv_dec_body,
        grid=(RWKV_HEADS,),
        in_specs=[_full((s, RWKV_SHIFT_W)), _full((s, RWKV_SHIFT_W)),
                  pl.BlockSpec((1, D, D, s), lambda h: (h, 0, 0, 0)),
                  col(RWKV_SHIFT_W), col(RWKV_W), _full((RWKV_W, D)), col(RWKV_W), _full((RWKV_W, D)),
                  col(RWKV_W), col(RWKV_W), col(RWKV_W), col(RWKV_W), col(RWKV_W)],
        out_specs=[_full((s, RWKV_W)), pl.BlockSpec((1, D, D, s), lambda h: (h, 0, 0, 0))],
        out_shape=[jax.ShapeDtypeStruct((s, RWKV_W), F32), jax.ShapeDtypeStruct((RWKV_HEADS, D, D, s), F32)],
        scratch_shapes=[vec_sc] * 7,
        compiler_params=_cp(("arbitrary",)),
        name="rwkv_decode",
    )(u_s, shift, state_t, *rw_cols)


def _memkv_body(m_ref, g_ref, w_ref, kn_ref, bd_ref, o_ref):
    x = m_ref[0]
    ms = jnp.mean(x * x, axis=-1, keepdims=True)
    xn = (x * lax.rsqrt(ms + RMS_EPS) * g_ref[...]).astype(BF16)
    kv = _nn(xn, w_ref[...])
    o_ref[0, :, 0:256] = _head_rms(kv[:, 0:256], kn_ref[...], bd_ref[...])
    o_ref[0, :, 256:512] = kv[:, 256:512]


def _mem_kv(mem, g, w, kn, bd):
    b, m, d = mem.shape
    return pl.pallas_call(
        _memkv_body,
        grid=(b,),
        in_specs=[pl.BlockSpec((1, m, d), lambda i: (i, 0, 0)), _full((1, d)), _full((d, 512)),
                  _full((1, 256)), _full((LANES, LANES))],
        out_specs=pl.BlockSpec((1, m, 512), lambda i: (i, 0, 0)),
        out_shape=jax.ShapeDtypeStruct((b, m, 512), F32),
        compiler_params=_cp(("arbitrary",)),
        name="mem_kv",
    )(mem, g, w, kn, bd)


def _mem_attn_body(qm_ref, kv_ref, o_ref):
    qm = qm_ref[...]
    kvb = kv_ref[0].astype(BF16)
    eye = _eye(LANES)
    heads = []
    for h in range(MEM_HEADS):
        col = h // 2
        kc = kvb[:, col * LANES:(col + 1) * LANES]
        vc = kvb[:, 256 + col * LANES:256 + (col + 1) * LANES]
        s = _nt(kc, qm[:, h * LANES:(h + 1) * LANES])
        m = jnp.max(s, axis=0, keepdims=True)
        p = jnp.exp2(s - m)
        p = p * (1.0 / jnp.sum(p, axis=0, keepdims=True))
        oT = _nn(_nt(eye, vc).astype(BF16), p.astype(BF16))
        heads.append(oT.T)
    lane = lax.broadcasted_iota(jnp.int32, heads[0].shape, 1)
    o_ref[...] = jnp.concatenate([jnp.where(lane < 64, heads[0], heads[1]),
                                  jnp.where(lane < 64, heads[2], heads[3])], axis=1)


def _mem_attn(qm, memkv, nb, rows_per_b, tb):
    n = qm.shape[0]
    nblk = rows_per_b // tb
    m = memkv.shape[1]
    return pl.pallas_call(
        _mem_attn_body,
        grid=(nb, nblk),
        in_specs=[pl.BlockSpec((tb, 512), lambda b, i: (b * nblk + i, 0)),
                  pl.BlockSpec((1, m, 512), lambda b, i: (b, 0, 0))],
        out_specs=pl.BlockSpec((tb, MEM_W), lambda b, i: (b * nblk + i, 0)),
        out_shape=jax.ShapeDtypeStruct((n, MEM_W), F32),
        compiler_params=_cp(("arbitrary", "arbitrary")),
        name="mem_attn",
    )(qm, memkv)


def _out_body(x_ref, on_ref, or_ref, om_ref, zs_ref, w_ref, y_ref):
    zs = zs_ref[...]
    cat = jnp.concatenate([on_ref[...] * zs[:, 0:512], or_ref[...] * zs[:, 512:768],
                           om_ref[...] * zs[:, 768:1024]], axis=1)
    y_ref[...] = x_ref[...] + _nn(cat.astype(BF16), w_ref[...])


def _out_proj(x2d, o_nsa, o_rw, o_mem, zs, w_out, tb):
    n, d = x2d.shape
    row = lambda i: (i, 0)
    return pl.pallas_call(
        _out_body,
        grid=(n // tb,),
        in_specs=[pl.BlockSpec((tb, d), row), pl.BlockSpec((tb, 512), row), pl.BlockSpec((tb, 256), row),
                  pl.BlockSpec((tb, 256), row), pl.BlockSpec((tb, 1024), row), _full((1024, d))],
        out_specs=pl.BlockSpec((tb, d), row),
        out_shape=jax.ShapeDtypeStruct((n, d), F32),
        compiler_params=_cp(("arbitrary",)),
        name="out_proj",
    )(x2d, o_nsa, o_rw, o_mem, zs, w_out)


PAGES_PER_STEP = 16
SKEW = CMP_STRIDE + 1


def _query_rows(qrow, g):
    rows = lax.broadcasted_iota(jnp.int32, (16, LANES), 0)
    qg = jnp.zeros((16, LANES), F32)
    for h in range(4):
        piece = qrow[:, (4 * g + h) * LANES:(4 * g + h + 1) * LANES].astype(F32)
        qg = jnp.where(rows == h, piece, qg)
    return qg.astype(BF16)


def _dec_a_body(pt_ref, *refs, n_ch, n_sb, n_sbp, p_len):
    pages = refs[:PAGES_PER_STEP]
    (q_ref, pelo_ref, pehi_ref, w1lo_ref, w1hi_ref, b1_ref, w2_ref, b2_ref, ckn_ref, cos_ref, sin_ref,
     bd_ref, ov_ref, oc_out, idx_out, xk, xv) = refs[PAGES_PER_STEP:]
    j = pl.program_id(1)
    cpp = PAGE_SIZE // CMP_STRIDE
    for i in range(PAGES_PER_STEP):
        r0 = pl.multiple_of((j * PAGES_PER_STEP + i) * cpp * SKEW, 8)
        for e, dst in ((0, xk), (1, xv)):
            tt = pages[i][0, e].T
            for c in range(cpp):
                dst[pl.ds(r0 + c * SKEW, CMP_STRIDE), :] = tt[c * CMP_STRIDE:(c + 1) * CMP_STRIDE, :]

    @pl.when(j == pl.num_programs(1) - 1)
    def _():
        lo = jnp.zeros((n_ch, 256), F32)
        hi = jnp.zeros((n_ch, 256), F32)
        for p in range(CMP_STRIDE):
            xp = jnp.concatenate([xk[pl.ds(p, n_ch, stride=SKEW), :], xv[pl.ds(p, n_ch, stride=SKEW), :]], axis=1)
            lo = lo + _nn((xp + pelo_ref[p:p + 1, :]).astype(BF16), w1lo_ref[p])
            hi = hi + _nn((xp + pehi_ref[p:p + 1, :]).astype(BF16), w1hi_ref[p])
        ck, cv = _compress_tail(lo, hi, n_ch, b1_ref, w2_ref, b2_ref, ckn_ref, cos_ref, sin_ref, bd_ref)
        ckb = ck.astype(BF16)
        cvb = cv.astype(BF16)
        qrow = q_ref[0]
        cend = lax.broadcasted_iota(jnp.int32, (1, n_ch), 1) * CMP_STRIDE + (CMP_BLK - 1)
        valid = cend <= p_len
        ocs = []
        imps = []
        for g in range(NSA_KV_HEADS):
            s = _nt(_query_rows(qrow, g), ckb)
            sm = jnp.where(valid, s, NEG)
            m = jnp.max(sm, axis=1, keepdims=True)
            p = jnp.where(valid, jnp.exp2(sm - m), 0.0)
            p = p * (1.0 / jnp.maximum(jnp.sum(p, axis=1, keepdims=True), 1e-30))
            ocs.append(_nn(p.astype(BF16), cvb))
            imps.append(p[0:1] + p[1:2] + p[2:3] + p[3:4])
        oc_out[0] = jnp.concatenate(ocs, axis=1)
        pz = jnp.concatenate(imps + [jnp.zeros((6, n_ch), F32)], axis=0)
        ph, plo = _split2(pz)
        imp = _nn(ph, ov_ref[...]) + _nn(plo, ov_ref[...])
        jj = lax.broadcasted_iota(jnp.int32, (8, n_sbp), 1)
        cur = p_len // SEL_BLK
        forced = (jj == 0) | (jj == cur) | (jj == cur - 1)
        score = jnp.where(jj < n_sb, jnp.where(forced, jnp.inf, imp), -jnp.inf)
        lane = lax.broadcasted_iota(jnp.int32, (8, LANES), 1)
        idx = jnp.zeros((8, LANES), jnp.int32)
        for n in range(N_SEL):
            mx = jnp.max(score, axis=1, keepdims=True)
            first = jnp.min(jnp.where(score == mx, jj, n_sbp), axis=1, keepdims=True)
            idx = jnp.where(lane == n, first, idx)
            score = jnp.where(jj == first, -jnp.inf, score)
        idx_out[0] = idx


def _decode_a(pt, pool_t, q3, cmp_args, ov, s, p_len):
    n_pages = p_len // PAGE_SIZE
    n_ch = p_len // CMP_STRIDE
    n_sb = p_len // SEL_BLK + 1
    n_sbp = ov.shape[1]
    steps = n_pages // PAGES_PER_STEP

    def page_spec(i):
        return pl.BlockSpec((1, 2, LANES, PAGE_SIZE),
                            lambda si, j, pt_ref: (pt_ref[si, j * PAGES_PER_STEP + i], 0, 0, 0))

    full = lambda shape: pl.BlockSpec(shape, lambda si, j, pt_ref: (0,) * len(shape))
    cmp_specs = [full((CMP_STRIDE, 256)), full((CMP_STRIDE, 256)), full((CMP_STRIDE, 256, 256)),
                 full((CMP_STRIDE, 256, 256)), full((1, 256)), full((256, 256)), full((1, 256)),
                 full((1, LANES)), full((n_ch, LANES)), full((n_ch, LANES)), full((LANES, LANES))]
    grid_spec = pltpu.PrefetchScalarGridSpec(
        num_scalar_prefetch=1,
        grid=(s, steps),
        in_specs=[page_spec(i) for i in range(PAGES_PER_STEP)]
        + [pl.BlockSpec((1, 1, 1024), lambda si, j, pt_ref: (si, 0, 0))] + cmp_specs + [full(ov.shape)],
        out_specs=[pl.BlockSpec((1, 16, 256), lambda si, j, pt_ref: (si, 0, 0)),
                   pl.BlockSpec((1, 8, LANES), lambda si, j, pt_ref: (si, 0, 0))],
        scratch_shapes=[pltpu.VMEM((n_ch * SKEW, LANES), F32), pltpu.VMEM((n_ch * SKEW, LANES), F32)],
    )
    return pl.pallas_call(
        functools.partial(_dec_a_body, n_ch=n_ch, n_sb=n_sb, n_sbp=n_sbp, p_len=p_len),
        grid_spec=grid_spec,
        out_shape=[jax.ShapeDtypeStruct((s, 16, 256), F32), jax.ShapeDtypeStruct((s, 8, LANES), jnp.int32)],
        compiler_params=_cp(("arbitrary", "arbitrary")),
        name="decode_select",
    )(pt, *([pool_t] * PAGES_PER_STEP), q3, *cmp_args, ov)


def _dec_b_body(pt_ref, idx_ref, pool_ref, q_ref, g_ref, oc_ref, new4_ref, neww_ref, win_ref, o_out, win_out,
                buf, sem, kall, vall, *, n_blk):
    nsel = NSA_KV_HEADS * N_SEL
    si = pl.program_id(0)
    n_seq = pl.num_programs(0)
    slot = lax.rem(si, 2)

    def page_copy(s, n, sl):
        jb = jnp.minimum(idx_ref[s, n], n_blk - 1)
        page = pt_ref[s, lax.shift_right_logical(jb, 1)]
        return pltpu.make_async_copy(pool_ref.at[page, pl.ds(2, 2)], buf.at[sl, n], sem.at[sl])

    @pl.when(si == 0)
    def _():
        for n in range(nsel):
            page_copy(0, n, 0).start()

    @pl.when(si + 1 < n_seq)
    def _():
        for n in range(nsel):
            page_copy(si + 1, n, 1 - slot).start()

    for n in range(nsel):
        page_copy(si, n, slot).wait()

    qrow = q_ref[0]
    gates = g_ref[0]
    oc = oc_ref[0]
    new4 = new4_ref[0]
    neww = neww_ref[0]

    wlen = win_ref.shape[3]
    r0 = lax.broadcasted_iota(jnp.int32, (LANES, LANES), 0)
    r1 = lax.broadcasted_iota(jnp.int32, (LANES, LANES), 1)
    lane_w = lax.broadcasted_iota(jnp.int32, (LANES, wlen), 1)
    new_t = []
    for c in range(2):
        col = jnp.sum(jnp.where(r0 == r1, neww[:, c * LANES:(c + 1) * LANES], 0.0), axis=1, keepdims=True)
        shifted = pltpu.roll(win_ref[0, c], wlen - 1, axis=1)
        nt = jnp.where(lane_w == wlen - 1, col, shifted)
        win_out[0, c] = nt
        new_t.append(nt.astype(BF16))
    kwT, vwT = new_t

    ksel_new = new4[:, 256:384].astype(BF16).astype(F32)
    vsel_new = new4[:, 384:512].astype(BF16).astype(F32)
    nk = N_SEL * PAGE_SIZE
    lane_k = lax.broadcasted_iota(jnp.int32, (16, nk), 1)

    heads = []
    for g in range(NSA_KV_HEADS):
        qgb = _query_rows(qrow, g)
        valid = lane_k < 0
        for n in range(N_SEL):
            kall[:, n * PAGE_SIZE:(n + 1) * PAGE_SIZE] = buf[slot, g * N_SEL + n, 0].astype(BF16)
            vall[:, n * PAGE_SIZE:(n + 1) * PAGE_SIZE] = buf[slot, g * N_SEL + n, 1].astype(BF16)
            jb = idx_ref[si, g * N_SEL + n]
            half = jb % 2
            valid = valid | ((jb < n_blk) & (lax.shift_right_logical(lane_k, 7) == n)
                             & ((lax.shift_right_logical(lane_k, 6) & 1) == half))
        s = _nn(qgb, kall[...])
        s_new = jnp.sum(qgb.astype(F32) * ksel_new, axis=1, keepdims=True)
        sm = jnp.where(valid, s, NEG)
        m = jnp.maximum(jnp.max(sm, axis=1, keepdims=True), s_new)
        p = jnp.where(valid, jnp.exp2(sm - m), 0.0)
        p_new = jnp.exp2(s_new - m)
        l = jnp.sum(p, axis=1, keepdims=True) + p_new
        o_s = (_nt(p.astype(BF16), vall[...]) + p_new.astype(BF16).astype(F32) * vsel_new) * (1.0 / l)

        sw = _nn(qgb, kwT)
        mw = jnp.max(sw, axis=1, keepdims=True)
        pw = jnp.exp2(sw - mw)
        o_w = _nt(pw.astype(BF16), vwT) * (1.0 / jnp.sum(pw, axis=1, keepdims=True))

        o_c = oc[:, g * LANES:(g + 1) * LANES]
        for h in range(4):
            hd = 4 * g + h
            mix = (gates[:, 3 * hd:3 * hd + 1] * o_c[h:h + 1, :] + gates[:, 3 * hd + 1:3 * hd + 2] * o_s[h:h + 1, :]
                   + gates[:, 3 * hd + 2:3 * hd + 3] * o_w[h:h + 1, :])
            if hd % 2 != g:
                mix = pltpu.roll(mix, 64, axis=1)
            heads.append(mix)
    lane1 = lax.broadcasted_iota(jnp.int32, (1, LANES), 1)
    cols = [jnp.where(lane1 < 64, heads[2 * c], heads[2 * c + 1]) for c in range(4)]
    o_out[0] = jnp.concatenate(cols, axis=1)


def _decode_b(pt, idx32, pool_t, q3, g3, oc, new4, neww, win_t, s, p_len):
    n_blk = p_len // SEL_BLK
    nsel = NSA_KV_HEADS * N_SEL
    wlen = win_t.shape[3]

    per = lambda shape: pl.BlockSpec(shape, lambda si, pt_ref, idx_ref: (si,) + (0,) * (len(shape) - 1))
    grid_spec = pltpu.PrefetchScalarGridSpec(
        num_scalar_prefetch=2,
        grid=(s,),
        in_specs=[pl.BlockSpec(memory_space=pl.ANY),
                  per((1, 1, 1024)), per((1, 1, LANES)), per((1, 16, 256)), per((1, 1, 512)), per((1, 1, 256)),
                  per((1, 2, LANES, wlen))],
        out_specs=[per((1, 1, 512)), per((1, 2, LANES, wlen))],
        scratch_shapes=[pltpu.VMEM((2, nsel, 2, LANES, PAGE_SIZE), F32), pltpu.SemaphoreType.DMA((2,)),
                        pltpu.VMEM((LANES, N_SEL * PAGE_SIZE), BF16), pltpu.VMEM((LANES, N_SEL * PAGE_SIZE), BF16)],
    )
    return pl.pallas_call(
        functools.partial(_dec_b_body, n_blk=n_blk),
        grid_spec=grid_spec,
        out_shape=[jax.ShapeDtypeStruct((s, 1, 512), F32), jax.ShapeDtypeStruct((s, 2, LANES, wlen), F32)],
        compiler_params=_cp(("arbitrary",)),
        name="decode_attend",
    )(pt, idx32, pool_t, q3, g3, oc, new4, neww, win_t)


def _mem_dec_body(qm_ref, kv_ref, o_ref):
    qrow = qm_ref[0]
    kT = kv_ref[0, 0].astype(BF16)
    vT = kv_ref[0, 1].astype(BF16)
    rows = lax.broadcasted_iota(jnp.int32, (16, 256), 0)
    lanes = lax.broadcasted_iota(jnp.int32, (16, 256), 1)
    qr = jnp.zeros((16, 256), F32)
    for h in range(MEM_HEADS):
        piece = qrow[:, h * LANES:(h + 1) * LANES].astype(F32)
        col = h // 2
        wide = jnp.concatenate([piece if c == col else jnp.zeros_like(piece) for c in range(2)], axis=1)
        qr = jnp.where(rows == h, wide, qr)
    s = _nn(qr.astype(BF16), kT)
    m = jnp.max(s, axis=1, keepdims=True)
    p = jnp.exp2(s - m)
    p = p * (1.0 / jnp.sum(p, axis=1, keepdims=True))
    of = _nt(p.astype(BF16), vT)
    keep = rows == lax.shift_right_logical(lanes, 6)
    o_ref[0] = jnp.sum(jnp.where(keep, of, 0.0), axis=0, keepdims=True)


def _mem_decode(qm3, mem_t):
    s, _, hd, m = mem_t.shape
    per = lambda shape: pl.BlockSpec(shape, lambda i: (i,) + (0,) * (len(shape) - 1))
    return pl.pallas_call(
        _mem_dec_body,
        grid=(s,),
        in_specs=[per((1, 1, 512)), per((1, 2, hd, m))],
        out_specs=per((1, 1, MEM_W)),
        out_shape=jax.ShapeDtypeStruct((s, 1, MEM_W), F32),
        compiler_params=_cp(("arbitrary",)),
        name="mem_decode",
    )(qm3, mem_t)


def _prep_w_in(w_in):
    d = w_in.shape[0]
    q_w, kv_w, g_w, zn_w, u_w, zr_w, qm_w, zm_w = jnp.split(
        w_in, [int(c) for c in np.cumsum([512, 768, 24, 512, 896, 256, 256])], axis=1)
    zero64 = jnp.zeros((d, HEAD_DIM), w_in.dtype)
    q_slots = []
    for hd in range(NSA_HEADS):
        wq = q_w[:, hd * HEAD_DIM:(hd + 1) * HEAD_DIM]
        q_slots += [wq, zero64] if hd // NSA_HPG == 0 else [zero64, wq]
    qm_slots = []
    for h in range(MEM_HEADS):
        wq = qm_w[:, h * HEAD_DIM:(h + 1) * HEAD_DIM]
        qm_slots += [wq, zero64] if h % 2 == 0 else [zero64, wq]
    g_pad = jnp.pad(g_w, ((0, 0), (0, LANES - g_w.shape[1])))
    w_p = jnp.concatenate(q_slots + [kv_w, zn_w, zr_w, zm_w, u_w] + qm_slots + [g_pad], axis=1)
    return w_p.astype(BF16)


def _rope_table(pos):
    half = HEAD_DIM // 2
    inv = ROPE_THETA ** (-2.0 * jnp.arange(half, dtype=F32) / HEAD_DIM)
    ang = pos.astype(F32)[:, None] * inv[None, :]
    cos = jnp.cos(ang)
    sin = jnp.sin(ang)
    return jnp.tile(jnp.concatenate([cos, cos], axis=1), (1, 2)), jnp.tile(jnp.concatenate([-sin, sin], axis=1), (1, 2))


def _block_diag(blocks):
    n = len(blocks)
    r, c = blocks[0].shape
    out = jnp.zeros((n * r, n * c), blocks[0].dtype)
    for i, blk in enumerate(blocks):
        out = out.at[i * r:(i + 1) * r, i * c:(i + 1) * c].set(blk)
    return out


def _prep_compress(cmp_pe, cmp_w1, cmp_b1, cmp_w2, cmp_b2, ck_norm, cend):
    eg = [(0, 0), (0, 1), (1, 0), (1, 1)]
    pelo = jnp.concatenate([cmp_pe[e, :CMP_STRIDE] for e, _ in eg], axis=1)
    pehi = jnp.concatenate([cmp_pe[e, CMP_STRIDE:] for e, _ in eg], axis=1)
    w1lo = jnp.stack([_block_diag([cmp_w1[e, p] for e, _ in eg]) for p in range(CMP_STRIDE)]).astype(BF16)
    w1hi = jnp.stack([_block_diag([cmp_w1[e, CMP_STRIDE + p] for e, _ in eg]) for p in range(CMP_STRIDE)]).astype(BF16)
    b1 = jnp.concatenate([cmp_b1[e] for e, _ in eg])[None, :]
    w2 = _block_diag([cmp_w2[e] for e, _ in eg]).astype(BF16)
    b2 = jnp.concatenate([cmp_b2[e] for e, _ in eg])[None, :]
    ckn = jnp.tile(ck_norm, 2)[None, :]
    cos_c, sin_c = _rope_table(cend)
    return [pelo, pehi, w1lo, w1hi, b1, w2, b2, ckn, cos_c, sin_c]


def _overlap(n_cb, n_sb):
    cstart = np.arange(n_cb)[:, None] * CMP_STRIDE
    sstart = np.arange(n_sb)[None, :] * SEL_BLK
    ov = np.clip(np.minimum(cstart + CMP_BLK, sstart + SEL_BLK) - np.maximum(cstart, sstart), 0, None)
    return ov.astype(np.float32) / CMP_BLK


def _pad_to(a, rows, cols):
    return np.pad(a, ((0, rows - a.shape[0]), (0, cols - a.shape[1])))


def kernel(x_prompt, x_sample, mem_prompt, cache_nsa, cache_win, cache_mem, state_rwkv_shift, state_rwkv_wkv,
           page_table, ln_g, w_in, nsa_q_norm, nsa_k_norm, cmp_pe, cmp_w1, cmp_b1, cmp_w2, cmp_b2, rwkv_mu, rwkv_w0,
           rwkv_w2, rwkv_a0, rwkv_a2, rwkv_k_k, rwkv_k_a, rwkv_r_k, rwkv_ln_w, rwkv_ln_b, mem_norm_g, w_mem_kv,
           mem_q_norm, mem_k_norm, w_out):
    depth = ln_g.shape[0]
    assert depth == 1
    bp, tp, d = x_prompt.shape
    bs, ts, _ = x_sample.shape
    assert ts == 1
    n_pages = page_table.shape[1]
    p_len = n_pages * PAGE_SIZE
    n_pool = cache_nsa.shape[1]
    wlen = cache_win.shape[2]
    mlen = cache_mem.shape[2]
    assert tp % KTILE == 0 and tp >= WINDOW + QBLK and n_pages % PAGES_PER_STEP == 0 and wlen == WINDOW
    l = 0

    bd = jnp.asarray(np.kron(np.eye(2), np.ones((HEAD_DIM, HEAD_DIM))), BF16)
    w_p = _prep_w_in(w_in[l])
    lng = ln_g[l][None, :]
    qn = jnp.tile(nsa_q_norm[l], 16)[None, :]
    kn = jnp.tile(nsa_k_norm[l], (1, 2))
    mqn = jnp.tile(mem_q_norm[l], 8)[None, :]
    w_out_b = w_out[l].astype(BF16)
    rw_vecs = [rwkv_mu[l], rwkv_w0[l], rwkv_a0[l], rwkv_k_k[l], rwkv_k_a[l], rwkv_r_k[l].reshape(RWKV_W),
               rwkv_ln_w[l], rwkv_ln_b[l]]
    mu, w0, a0, kkp, kap, rkp, lnw, lnb = rw_vecs
    rw_args = [mu[None, :], w0[None, :], rwkv_w2[l], a0[None, :], rwkv_a2[l], kkp[None, :], kap[None, :],
               rkp[None, :], lnw[None, :], lnb[None, :]]
    rw_cols = [mu[:, None], w0[:, None], rwkv_w2[l].T, a0[:, None], rwkv_a2[l].T, kkp[:, None], kap[:, None],
               rkp[:, None], lnw[:, None], lnb[:, None]]

    tb = 512
    xp2 = x_prompt.reshape(bp * tp, d)
    cos_p, sin_p = _rope_table(jnp.arange(tp))
    q, rows4, rows4t, rows_w, kvb, vt, zs, u_rw, qm, gates = _project(
        xp2, bp, tp, lng, w_p, cos_p, sin_p, tp // tb, qn, kn, mqn, bd, tb)

    n_ch = tp // CMP_STRIDE
    cend_p = jnp.arange(n_ch) * CMP_STRIDE + CMP_BLK - 1
    cmp_args_p = _prep_compress(cmp_pe[l], cmp_w1[l], cmp_b1[l], cmp_w2[l], cmp_b2[l], nsa_k_norm[l, 0], cend_p) + [bd]
    ck, cv = _compress_prompt(rows4, cmp_args_p, bp, tp)
    n_sb = tp // SEL_BLK
    n_sbp = -(-n_sb // LANES) * LANES
    ov = _overlap(n_ch - 1, n_sb)
    ovt = jnp.asarray(_pad_to(ov.T, n_sbp, n_ch), BF16)
    o_nsa = _nsa_prompt(q, gates, kvb, vt, ck, cv, ovt, bp, tp)

    nbat = max(n for n in (8, 4, 2, 1) if bp % n == 0)
    o_rw, wkv_p = _rwkv_prompt(u_rw.reshape(bp, tp, RWKV_SHIFT_W), rw_args, bd, nbat)
    o_rw = o_rw.reshape(bp * tp, RWKV_W)

    kmn = jnp.tile(mem_k_norm[l], 4)[None, :]
    memkv_p = _mem_kv(mem_prompt, mem_norm_g[l][None, :], w_mem_kv[l].astype(BF16), kmn, bd)
    o_mem = _mem_attn(qm, memkv_p, bp, tp, 512)
    y_prompt = _out_proj(xp2, o_nsa, o_rw, o_mem, zs, w_out_b, tb).reshape(bp, tp, d)

    nsa_rows_prompt = rows4t.reshape(bp, 4, NSA_KV_HEADS, HEAD_DIM, tp).transpose(0, 4, 1, 2, 3)[None]
    win_p = min(WINDOW, tp)
    win_prompt = rows_w.reshape(bp, tp, 2, NSA_KV_HEADS, HEAD_DIM)[None, :, tp - win_p:]
    shift_prompt = u_rw.reshape(bp, tp, RWKV_SHIFT_W)[None, :, -1]
    mem_kv_prompt = memkv_p.reshape(1, bp, mlen, 2, MEM_HEADS, HEAD_DIM)

    xs2 = x_sample.reshape(bs, d)
    cos_s, sin_s = _rope_table(jnp.full((bs,), p_len))
    q_s, rows4_s, rows4t_s, rows_w_s, _, _, zs_s, u_s, qm_s, gates_s = _project(
        xs2, 1, bs, lng, w_p, cos_s, sin_s, 1, qn, kn, mqn, bd, bs)

    n_ch_s = p_len // CMP_STRIDE
    cend_s = jnp.arange(n_ch_s) * CMP_STRIDE + CMP_BLK - 1
    cmp_args_s = _prep_compress(cmp_pe[l], cmp_w1[l], cmp_b1[l], cmp_w2[l], cmp_b2[l], nsa_k_norm[l, 0], cend_s) + [bd]
    n_sb_s = p_len // SEL_BLK + 1
    n_sbp_s = -(-n_sb_s // LANES) * LANES
    ov_s = jnp.asarray(_pad_to(_overlap(n_ch_s - 1, n_sb_s), n_ch_s, n_sbp_s), BF16)
    pool_t = cache_nsa[l].transpose(0, 2, 3, 4, 1).reshape(n_pool, 4, NSA_KV_HEADS * HEAD_DIM, PAGE_SIZE)
    oc, idx = _decode_a(page_table, pool_t, q_s.reshape(bs, 1, 1024), cmp_args_s, ov_s, bs, p_len)
    idx32 = idx[:, 0:NSA_KV_HEADS, 0:N_SEL].reshape(bs, NSA_KV_HEADS * N_SEL)
    win_t = cache_win[l].transpose(0, 2, 3, 4, 1).reshape(bs, 2, NSA_KV_HEADS * HEAD_DIM, wlen)
    o_nsa_s, win_s = _decode_b(page_table, idx32, pool_t, q_s.reshape(bs, 1, 1024), gates_s.reshape(bs, 1, LANES), oc,
                               rows4_s.reshape(bs, 1, 512), rows_w_s.reshape(bs, 1, 256), win_t, bs, p_len)

    state_t = state_rwkv_wkv[l].transpose(1, 2, 3, 0)
    o_rw_s, wkv_s_t = _rwkv_decode(u_s, state_rwkv_shift[l], state_t, rw_cols)

    mem_t = cache_mem[l].transpose(0, 2, 3, 4, 1).reshape(bs, 2, MEM_HEADS * HEAD_DIM, mlen)
    o_mem_s = _mem_decode(qm_s.reshape(bs, 1, 512), mem_t).reshape(bs, MEM_W)
    y_sample = _out_proj(xs2, o_nsa_s.reshape(bs, 512), o_rw_s, o_mem_s, zs_s, w_out_b, bs).reshape(bs, 1, d)

    nsa_rows_sample = rows4t_s.reshape(4, NSA_KV_HEADS, HEAD_DIM, bs).transpose(3, 0, 1, 2)[None, :, None]
    win_sample = win_s.reshape(bs, 2, NSA_KV_HEADS, HEAD_DIM, wlen).transpose(0, 4, 1, 2, 3)[None]
    shift_sample = u_s[None]
    wkv_sample = wkv_s_t.transpose(3, 0, 1, 2)[None]
    return (y_prompt, y_sample, nsa_rows_prompt, win_prompt, shift_prompt, wkv_p[None], mem_kv_prompt,
            nsa_rows_sample, win_sample, shift_sample, wkv_sample)
```

```python
import functools

import numpy as np
import jax
import jax.numpy as jnp
from jax import lax
from jax.experimental import pallas as pl
from jax.experimental.pallas import tpu as pltpu

F32 = jnp.float32
BF16 = jnp.bfloat16

HEAD_DIM = 64
NSA_HEADS = 8
NSA_KV_HEADS = 2
NSA_HPG = 4
RWKV_HEADS = 4
MEM_HEADS = 4
NSA_W = 512
RWKV_W = 256
MEM_W = 256
CMP_BLK = 32
CMP_STRIDE = 16
SEL_BLK = 64
N_SEL = 16
WINDOW = 512
PAGE_SIZE = 128
RWKV_SHIFT_W = 896
ROPE_THETA = 10000.0
RMS_EPS = 1e-6
GN_EPS = 64e-5
SCALE = HEAD_DIM ** -0.5
LOG2E = 1.4426950408889634

LANES = 128
QBLK = 128
KTILE = 1024
RWKV_CHUNK = 64
NEG = -1e30

C_Q = 0
C_KV = 1024
C_Z = 1792
C_U = 2816
C_QM = 3712
C_G = 4224
N_INP = 4352

VMEM_LIMIT = 56 * 1024 * 1024


def _cp(sem, flags=None):
    return pltpu.CompilerParams(dimension_semantics=sem, vmem_limit_bytes=VMEM_LIMIT, flags=flags)


def _nt(a, b):
    return lax.dot_general(a, b, (((1,), (1,)), ((), ())), preferred_element_type=F32)


def _nn(a, b):
    return jnp.dot(a, b, preferred_element_type=F32)


def _nnb(a, b):
    return _nn(a.astype(BF16), b.astype(BF16))


def _ntb(a, b):
    return _nt(a.astype(BF16), b.astype(BF16))


def _split2(x):
    hi = x.astype(BF16)
    lo = (x - hi.astype(F32)).astype(BF16)
    return hi, lo


def _split3(x):
    hi = x.astype(BF16)
    r = x - hi.astype(F32)
    mid = r.astype(BF16)
    lo = (r - mid.astype(F32)).astype(BF16)
    return hi, mid, lo


def _nn_x(a, b):
    ah, al = _split2(a)
    bh, bl = _split2(b)
    return _nn(ah, bh) + _nn(ah, bl) + _nn(al, bh)


def _segsum(x, bd):
    cols = []
    for c in range(x.shape[1] // LANES):
        hi, lo = _split2(x[:, c * LANES:(c + 1) * LANES])
        cols.append(_nn(hi, bd) + _nn(lo, bd))
    return cols[0] if len(cols) == 1 else jnp.concatenate(cols, axis=1)


def _rot_half(x):
    lane = lax.broadcasted_iota(jnp.int32, x.shape, 1)
    up = pltpu.roll(x, 96, axis=1)
    dn = pltpu.roll(x, 32, axis=1)
    return jnp.where((lane & 63) < 32, up, dn)


def _rope_cols(v, cosf, sinf):
    cols = []
    for c in range(v.shape[1] // LANES):
        xc = v[:, c * LANES:(c + 1) * LANES]
        cols.append(xc * cosf + _rot_half(xc) * sinf)
    return cols[0] if len(cols) == 1 else jnp.concatenate(cols, axis=1)


def _head_rms(v, gain, bd):
    ss = _segsum(v * v, bd)
    return v * lax.rsqrt(ss * (1.0 / HEAD_DIM) + RMS_EPS) * gain


def _sigmoid(x):
    return 1.0 / (1.0 + jnp.exp(-x))


def _eye(n, dtype=BF16):
    r0 = lax.broadcasted_iota(jnp.int32, (n, n), 0)
    r1 = lax.broadcasted_iota(jnp.int32, (n, n), 1)
    return jnp.where(r0 == r1, 1.0, 0.0).astype(dtype)


def _softmax2_cols(s, bias, col_ok=None):
    ps = []
    for h in range(s.shape[1] // LANES):
        sm = s[:, h * LANES:(h + 1) * LANES] + bias
        m = jnp.max(sm, axis=0, keepdims=True)
        p = jnp.exp2(sm - m)
        r = 1.0 / jnp.sum(p, axis=0, keepdims=True)
        if col_ok is not None:
            r = jnp.where(col_ok, r, 0.0)
        ps.append(p * r)
    return jnp.concatenate(ps, axis=1)


def _exp2_cols_bf16(s, bias):
    ps = []
    for h in range(s.shape[1] // LANES):
        sm = s[:, h * LANES:(h + 1) * LANES] + bias
        ps.append(jnp.exp2((sm - jnp.max(sm, axis=0, keepdims=True)).astype(BF16)))
    return jnp.concatenate(ps, axis=1)


def _with_ones(vt):
    return jnp.concatenate([vt, jnp.ones((16, vt.shape[1]), BF16)], axis=0)


def _full(shape):
    nd = len(shape)
    return pl.BlockSpec(shape, lambda *_: (0,) * nd)


def _proj_body(x_ref, lng_ref, w_ref, cos_ref, sin_ref, qn_ref, kn_ref, mqn_ref, bd_ref,
               q_out, rows4_out, rows4t_out, rowsw_out, kvb_out, vt_out, zs_out, u_out, qm_out, g_out):
    x = x_ref[...]
    ms = jnp.mean(x * x, axis=-1, keepdims=True)
    xn = (x * lax.rsqrt(ms + RMS_EPS) * lng_ref[...]).astype(BF16)
    bd = bd_ref[...]
    cosf = cos_ref[...]
    sinf = sin_ref[...]

    def seg(a, b):
        return jnp.dot(xn, w_ref[:, a:b], preferred_element_type=F32)

    q = _rope_cols(_head_rms(seg(C_Q, C_KV), qn_ref[...], bd), cosf, sinf) * (SCALE * LOG2E)
    q_out[...] = q.astype(BF16)

    kv = seg(C_KV, C_Z)
    kn = kn_ref[...]
    ksel = _rope_cols(_head_rms(kv[:, 256:384], kn[1:2], bd), cosf, sinf)
    kwin = _rope_cols(_head_rms(kv[:, 512:640], kn[2:3], bd), cosf, sinf)
    rows4 = jnp.concatenate([kv[:, 0:256], ksel, kv[:, 384:512]], axis=1)
    rows4_out[...] = rows4
    rows4t_out[0] = rows4.T
    rowsw_out[:, 0:128] = kwin
    rowsw_out[:, 128:256] = kv[:, 640:768]
    kvb_out[:, 0:128] = ksel.astype(BF16)
    kvb_out[:, 128:256] = kv[:, 384:512].astype(BF16)
    kvb_out[:, 256:384] = kwin.astype(BF16)
    kvb_out[:, 384:512] = kv[:, 640:768].astype(BF16)
    vt = jnp.concatenate([kv[:, 384:512], kv[:, 640:768]], axis=1).T
    vt_out[0, 0] = vt[0:128].astype(BF16)
    vt_out[0, 1] = vt[128:256].astype(BF16)

    z = seg(C_Z, C_U)
    zs_out[...] = z * _sigmoid(z)
    u_out[...] = seg(C_U, C_QM)
    qm = _head_rms(seg(C_QM, C_G), mqn_ref[...], bd) * (SCALE * LOG2E)
    qm_out[...] = qm.astype(BF16)
    g_out[...] = _sigmoid(seg(C_G, N_INP))


def _project(x2d, nb, t, lng, w_p, cos_t, sin_t, n_tab_blocks, qn, kn, mqn, bd, tb):
    n, d = x2d.shape
    nblk = t // tb
    row = lambda i: (i, 0)
    tab = lambda i: (i % n_tab_blocks, 0)
    outs = [
        jax.ShapeDtypeStruct((n, 1024), BF16),
        jax.ShapeDtypeStruct((n, 512), F32),
        jax.ShapeDtypeStruct((nb, 512, t), F32),
        jax.ShapeDtypeStruct((n, 256), F32),
        jax.ShapeDtypeStruct((n, 512), BF16),
        jax.ShapeDtypeStruct((nb, 2, LANES, t), BF16),
        jax.ShapeDtypeStruct((n, 1024), F32),
        jax.ShapeDtypeStruct((n, RWKV_SHIFT_W), F32),
        jax.ShapeDtypeStruct((n, 512), BF16),
        jax.ShapeDtypeStruct((n, 128), F32),
    ]
    out_specs = [pl.BlockSpec((tb, o.shape[1]), row) for o in outs]
    out_specs[2] = pl.BlockSpec((1, 512, tb), lambda i: (i // nblk, 0, i % nblk))
    out_specs[5] = pl.BlockSpec((1, 2, LANES, tb), lambda i: (i // nblk, 0, 0, i % nblk))
    return pl.pallas_call(
        _proj_body,
        grid=(n // tb,),
        in_specs=[
            pl.BlockSpec((tb, d), row),
            _full((1, d)),
            _full((d, N_INP)),
            pl.BlockSpec((tb, LANES), tab),
            pl.BlockSpec((tb, LANES), tab),
            _full((1, 1024)),
            _full((3, LANES)),
            _full((1, 512)),
            _full((LANES, LANES)),
        ],
        out_specs=out_specs,
        out_shape=outs,
        compiler_params=_cp(("arbitrary",)),
        name="proj",
    )(x2d, lng, w_p, cos_t, sin_t, qn, kn, mqn, bd)


def _compress_tail(lo, hi, n_ch, b1_ref, w2_ref, b2_ref, ckn_ref, cos_ref, sin_ref, bd_ref):
    his = pltpu.roll(hi, n_ch - 1, axis=0)
    pre = lo + his + b1_ref[...]
    hid = pre * _sigmoid(pre)
    out2 = _nn(hid.astype(BF16), w2_ref[...]) + b2_ref[...]
    ck = _head_rms(out2[:, 0:128], ckn_ref[...], bd_ref[...])
    ck = _rope_cols(ck, cos_ref[...], sin_ref[...])
    return ck, out2[:, 128:256]


def _compress_compute(xk_ref, xv_ref, n_ch, pelo_ref, pehi_ref, w1lo_ref, w1hi_ref, b1_ref, w2_ref, b2_ref,
                      ckn_ref, cos_ref, sin_ref, bd_ref):
    lo = jnp.zeros((n_ch, 256), F32)
    hi = jnp.zeros((n_ch, 256), F32)
    for p in range(CMP_STRIDE):
        xp = jnp.concatenate([xk_ref[pl.ds(p, n_ch, stride=CMP_STRIDE), :],
                              xv_ref[pl.ds(p, n_ch, stride=CMP_STRIDE), :]], axis=1)
        lo = lo + _nn((xp + pelo_ref[p:p + 1, :]).astype(BF16), w1lo_ref[p])
        hi = hi + _nn((xp + pehi_ref[p:p + 1, :]).astype(BF16), w1hi_ref[p])
    return _compress_tail(lo, hi, n_ch, b1_ref, w2_ref, b2_ref, ckn_ref, cos_ref, sin_ref, bd_ref)


def _compress_body(xk_ref, xv_ref, pelo_ref, pehi_ref, w1lo_ref, w1hi_ref, b1_ref, w2_ref, b2_ref,
                   ckn_ref, cos_ref, sin_ref, bd_ref, ck_out, cv_out, *, n_ch):
    ck, cv = _compress_compute(xk_ref, xv_ref, n_ch, pelo_ref, pehi_ref, w1lo_ref, w1hi_ref, b1_ref, w2_ref,
                               b2_ref, ckn_ref, cos_ref, sin_ref, bd_ref)
    ck_out[0] = ck.astype(BF16)
    cv_out[0] = cv.T.astype(BF16)


def _cmp_specs(n_ch):
    return [
        _full((CMP_STRIDE, 256)), _full((CMP_STRIDE, 256)),
        _full((CMP_STRIDE, 256, 256)), _full((CMP_STRIDE, 256, 256)),
        _full((1, 256)), _full((256, 256)), _full((1, 256)),
        _full((1, LANES)), _full((n_ch, LANES)), _full((n_ch, LANES)), _full((LANES, LANES)),
    ]


def _compress_prompt(rows4, cmp_args, b, t):
    n_ch = t // CMP_STRIDE
    outs = [jax.ShapeDtypeStruct((b, n_ch, LANES), BF16), jax.ShapeDtypeStruct((b, LANES, n_ch), BF16)]
    return pl.pallas_call(
        functools.partial(_compress_body, n_ch=n_ch),
        grid=(b,),
        in_specs=[pl.BlockSpec((t, LANES), lambda i: (i, 0)), pl.BlockSpec((t, LANES), lambda i: (i, 1))]
        + _cmp_specs(n_ch),
        out_specs=[pl.BlockSpec((1, n_ch, LANES), lambda i: (i, 0, 0)), pl.BlockSpec((1, LANES, n_ch), lambda i: (i, 0, 0))],
        out_shape=outs,
        compiler_params=_cp(("arbitrary",)),
        name="compress_prompt",
    )(rows4, rows4, *cmp_args)


def _select_blocks(score, n_sel):
    jj = lax.broadcasted_iota(jnp.int32, score.shape, 0)
    big = jnp.int32(score.shape[0])
    sel = jnp.zeros(score.shape, F32)
    for _ in range(n_sel):
        mx = jnp.max(score, axis=0, keepdims=True)
        first = jnp.min(jnp.where(score == mx, jj, big), axis=0, keepdims=True)
        pick = jj == first
        sel = jnp.where(pick, 1.0, sel)
        score = jnp.where(pick, -jnp.inf, score)
    return sel


def _nsa_prompt_body(q_ref, g_ref, kvb_ref, vt_ref, ck_ref, cvt_ref, ovt_ref, o_ref, sel_sc, *, n_cb, n_sb):
    i = pl.program_id(1)
    start = i * QBLK
    nh = NSA_HEADS
    q = q_ref[...]
    qall = jnp.concatenate([q[:, hd * LANES:(hd + 1) * LANES] for hd in range(nh)], axis=0)
    gT = g_ref[...].T
    qpos = start + lax.broadcasted_iota(jnp.int32, (1, QBLK), 1)
    qpos2 = jnp.concatenate([qpos, qpos], axis=1)
    wk = WINDOW + QBLK
    n_tiles = (start + QBLK + KTILE - 1) // KTILE
    lo_w = pl.multiple_of(jnp.maximum(start - WINDOW, 0), QBLK)

    cend = lax.broadcasted_iota(jnp.int32, (n_cb, 1), 0) * CMP_STRIDE + (CMP_BLK - 1)
    bias_c = jnp.where(cend <= qpos, 0.0, NEG)
    p = _softmax2_cols(_nt(ck_ref[0], qall), bias_c, qpos >= CMP_BLK - 1)
    cvt = cvt_ref[0]
    ocT, psums = [], []
    for g in range(NSA_KV_HEADS):
        pg = p[:, g * 512:(g + 1) * 512]
        ocT.append(_nn(cvt[g * 64:(g + 1) * 64], pg.astype(BF16)))
        psums.append(pg[:, 0:128] + pg[:, 128:256] + pg[:, 256:384] + pg[:, 384:512])
    ph, plo = _split2(jnp.concatenate(psums, axis=1))
    imp = _nn(ovt_ref[...], ph) + _nn(ovt_ref[...], plo)

    jj = lax.broadcasted_iota(jnp.int32, (n_sb, 2 * QBLK), 0)
    cur = lax.shift_right_logical(qpos2, 6)
    forced = (jj == 0) | (jj == cur) | (jj == cur - 1)
    score = jnp.where(jj * SEL_BLK <= qpos2, jnp.where(forced, jnp.inf, imp), -jnp.inf)
    sel_sc[...] = _select_blocks(score, N_SEL)

    bpt = KTILE // SEL_BLK

    def tile_step(t, carry):
        m, acc0, acc1 = carry
        k0 = pl.multiple_of(t * KTILE, KTILE)
        s = _nt(kvb_ref[pl.ds(k0, KTILE), 0:128], qall)
        sel8 = sel_sc[pl.ds(pl.multiple_of(t * bpt, bpt), bpt), :]
        member = jnp.concatenate([jnp.broadcast_to(sel8[b:b + 1, :], (SEL_BLK, 2 * QBLK)) for b in range(bpt)], axis=0)
        kpos = k0 + lax.broadcasted_iota(jnp.int32, (KTILE, 1), 0)
        bias = jnp.where((member > 0.5) & (kpos <= qpos2), 0.0, NEG).astype(BF16)
        vT = vt_ref[0, 0, :, pl.ds(k0, KTILE)]
        ms, als, ps = [], [], []
        for hd in range(nh):
            g = hd // NSA_HPG
            cs = slice(hd * LANES, (hd + 1) * LANES)
            sm = s[:, cs].astype(BF16) + bias[:, g * LANES:(g + 1) * LANES]
            mn = jnp.maximum(m[:, cs], jnp.max(sm, axis=0, keepdims=True).astype(F32))
            ms.append(mn)
            als.append(jnp.exp2(m[:, cs] - mn))
            ps.append(jnp.exp2(sm - mn.astype(BF16)))
        acc0 = jnp.concatenate(als[0:4], axis=1) * acc0 + _nn(_with_ones(vT[0:64]), jnp.concatenate(ps[0:4], axis=1))
        acc1 = jnp.concatenate(als[4:8], axis=1) * acc1 + _nn(_with_ones(vT[64:128]), jnp.concatenate(ps[4:8], axis=1))
        return jnp.concatenate(ms, axis=1), acc0, acc1

    m0 = jnp.full((1, nh * QBLK), NEG, F32)
    a0 = jnp.zeros((HEAD_DIM + 16, 4 * QBLK), F32)
    _, acc0, acc1 = lax.fori_loop(0, n_tiles, tile_step, (m0, a0, a0))
    osT = [a[0:HEAD_DIM] * (1.0 / a[HEAD_DIM:HEAD_DIM + 1]) for a in (acc0, acc1)]

    kpos_w = lo_w + lax.broadcasted_iota(jnp.int32, (wk, 1), 0)
    bias_w = jnp.where((kpos_w <= qpos) & (kpos_w > qpos - WINDOW), 0.0, NEG)
    pw = _exp2_cols_bf16(_nt(kvb_ref[pl.ds(lo_w, wk), 256:384], qall), bias_w)
    vwT = vt_ref[0, 1, :, pl.ds(lo_w, wk)]
    owT = []
    for g in range(NSA_KV_HEADS):
        a = _nn(_with_ones(vwT[g * 64:(g + 1) * 64]), pw[:, g * 512:(g + 1) * 512])
        owT.append(a[0:HEAD_DIM] * (1.0 / a[HEAD_DIM:HEAD_DIM + 1]))

    cols = []
    for c in range(4):
        g = c // 2
        parts = []
        for hd in (2 * c, 2 * c + 1):
            cs = slice((hd % 4) * QBLK, (hd % 4 + 1) * QBLK)
            parts.append(gT[3 * hd:3 * hd + 1, :] * ocT[g][:, cs] + gT[3 * hd + 1:3 * hd + 2, :] * osT[g][:, cs]
                         + gT[3 * hd + 2:3 * hd + 3, :] * owT[g][:, cs])
        cols.append(jnp.concatenate(parts, axis=0).T)
    o_ref[...] = jnp.concatenate(cols, axis=1)


def _nsa_prompt(q, gates, kvb, vt, ck, cvt, ovt, b, t):
    nq = t // QBLK
    n_cb = ck.shape[1]
    n_sb = ovt.shape[0]
    return pl.pallas_call(
        functools.partial(_nsa_prompt_body, n_cb=n_cb, n_sb=n_sb),
        grid=(b, nq),
        in_specs=[
            pl.BlockSpec((QBLK, 1024), lambda bi, i: (bi * nq + i, 0)),
            pl.BlockSpec((QBLK, LANES), lambda bi, i: (bi * nq + i, 0)),
            pl.BlockSpec((t, 512), lambda bi, i: (bi, 0)),
            pl.BlockSpec((1, 2, LANES, t), lambda bi, i: (bi, 0, 0, 0)),
            pl.BlockSpec((1, n_cb, LANES), lambda bi, i: (bi, 0, 0)),
            pl.BlockSpec((1, LANES, n_cb), lambda bi, i: (bi, 0, 0)),
            _full(ovt.shape),
        ],
        out_specs=pl.BlockSpec((QBLK, 512), lambda bi, i: (bi * nq + i, 0)),
        out_shape=jax.ShapeDtypeStruct((b * t, 512), F32),
        scratch_shapes=[pltpu.VMEM((n_sb, 2 * QBLK), F32)],
        compiler_params=_cp(("arbitrary", "arbitrary")),
        name="nsa_prompt",
    )(q, gates, kvb, vt, ck, cvt, ovt)


def _rwkv_prep(u, mu, w0, w2, a0, a2, kkp, kap, bd):
    r = u[:, 0:256]
    k = u[:, 256:512]
    v = u[:, 512:768]
    ww = w0 + _nn_x(jnp.tanh(u[:, 768:832]), w2)
    nw = -ww
    sp = jnp.maximum(nw, 0.0) + jnp.log(1.0 + jnp.exp(-jnp.abs(nw)))
    logdec = -jnp.exp(-sp - 0.5)
    a = _sigmoid(a0 + _nn_x(u[:, 832:896], a2))
    kk = k * kkp
    kk = kk / jnp.maximum(jnp.sqrt(_segsum(kk * kk, bd)), 1e-12)
    k2 = k * (1.0 + (a - 1.0) * kap)
    return r, k2, v, kk, kk * a, logdec


def _bmm_nt(a, b):
    return lax.dot_general(a, b, (((2,), (2,)), ((0,), (0,))), preferred_element_type=F32)


def _bmm(a, b):
    return lax.dot_general(a, b, (((2,), (1,)), ((0,), (0,))), preferred_element_type=F32)


def _rwkv_body(u_ref, mu_ref, w0_ref, w2_ref, a0_ref, a2_ref, kk_ref, ka_ref, rk_ref,
               lnw_ref, lnb_ref, bd_ref, o_ref, sout_ref, s_sc, prev_sc, *, nbat, n_dbl):
    c = pl.program_id(1)
    C = RWKV_CHUNK
    D = HEAD_DIM
    R = nbat * C
    G = nbat * RWKV_HEADS

    @pl.when(c == 0)
    def _():
        s_sc[...] = jnp.zeros(s_sc.shape, F32)
        prev_sc[...] = jnp.zeros(prev_sc.shape, F32)

    bd = bd_ref[...]
    rowi = lax.broadcasted_iota(jnp.int32, (C, 1), 0)
    us = []
    for n in range(nbat):
        ucur = u_ref[n]
        prev = jnp.where(rowi == 0, prev_sc[n], pltpu.roll(ucur, 1, axis=0))
        prev_sc[n] = ucur[C - 1:C, :]
        us.append(ucur + (prev - ucur) * mu_ref[...])
    u = us[0] if nbat == 1 else jnp.concatenate(us, axis=0)
    r, k2, v, kk, bb, logdec = _rwkv_prep(u, mu_ref[...], w0_ref[...], w2_ref[...], a0_ref[...], a2_ref[...],
                                          kk_ref[...], ka_ref[...], bd)

    ri = lax.broadcasted_iota(jnp.int32, (R, R), 0)
    ci = lax.broadcasted_iota(jnp.int32, (R, R), 1)
    same = lax.shift_right_logical(ri, 6) == lax.shift_right_logical(ci, 6)
    tri = jnp.where(same & (ri >= ci), 1.0, 0.0).astype(BF16)
    ones = jnp.where(same, 1.0, 0.0).astype(BF16)
    d1, d2, d3 = _split3(logdec)
    lcum = _nn(tri, d1) + _nn(tri, d2) + _nn(tri, d3)
    lend = _nn(ones, d1) + _nn(ones, d2) + _nn(ones, d3)
    e_neg = jnp.exp(-lcum)
    e_end = jnp.exp(lend - lcum)
    gend = jnp.exp(lend)

    def grp(x):
        return jnp.stack([x[n * C:(n + 1) * C, h * D:(h + 1) * D] for n in range(nbat) for h in range(RWKV_HEADS)])

    ar = jnp.concatenate([grp(-kk * jnp.exp(lcum - logdec)), grp(r * jnp.exp(lcum))], axis=1).astype(BF16)
    bk = jnp.concatenate([grp(bb * e_neg), grp(k2 * e_neg)], axis=1).astype(BF16)
    bkg = jnp.concatenate([grp(bb * e_end), grp(k2 * e_end)], axis=1).astype(BF16)
    vg = grp(v).astype(BF16)
    gg = jnp.stack([gend[n * C:n * C + 1, h * D:(h + 1) * D] for n in range(nbat) for h in range(RWKV_HEADS)])

    ti = lax.broadcasted_iota(jnp.int32, (1, C, C), 1)
    ii = lax.broadcasted_iota(jnp.int32, (1, C, C), 2)
    strict = ti > ii
    incl = ti >= ii
    s_old = s_sc[...]
    mm = _bmm_nt(ar, bk)
    lab = jnp.where(strict, mm[:, 0:C, 0:C], 0.0)
    lak = jnp.where(strict, mm[:, 0:C, C:2 * C], 0.0)
    mrb = jnp.where(incl, mm[:, C:2 * C, 0:C], 0.0)
    mrk = jnp.where(incl, mm[:, C:2 * C, C:2 * C], 0.0)
    ars = _bmm_nt(ar, s_old.astype(BF16))
    rhs = ars[:, 0:C] + _bmm(lak.astype(BF16), vg)
    x = jnp.where(ti == ii, 1.0, 0.0) + lab
    pw = lab
    for _ in range(n_dbl):
        pwb = pw.astype(BF16)
        pw = _bmm(pwb, pwb)
        x = x + _bmm(pw.astype(BF16), x.astype(BF16))
    uu = _bmm(x.astype(BF16), rhs.astype(BF16))
    yg = ars[:, C:2 * C] + _bmm(mrb.astype(BF16), uu.astype(BF16)) + _bmm(mrk.astype(BF16), vg)
    uv = jnp.concatenate([uu.astype(BF16), vg], axis=1)
    eye_g = jnp.broadcast_to(_eye(D)[None], (G, D, D))
    uvt = _bmm_nt(eye_g, uv).astype(BF16)
    s_sc[...] = s_old * gg + _bmm(uvt, bkg)

    y = jnp.concatenate([jnp.concatenate([yg[n * RWKV_HEADS + h] for h in range(RWKV_HEADS)], axis=1)
                         for n in range(nbat)], axis=0)
    mean = _segsum(y, bd) * (1.0 / D)
    dy = y - mean
    var = _segsum(dy * dy, bd) * (1.0 / D)
    yn = dy * lax.rsqrt(var + GN_EPS) * lnw_ref[...] + lnb_ref[...]
    out = yn + _segsum(r * k2 * rk_ref[...], bd) * v
    for n in range(nbat):
        o_ref[n] = out[n * C:(n + 1) * C]

    @pl.when(c == pl.num_programs(1) - 1)
    def _():
        for n in range(nbat):
            sout_ref[n] = s_sc[n * RWKV_HEADS:(n + 1) * RWKV_HEADS]


def _rwkv_prompt(u3, rw_args, bd, nbat):
    b, t, _ = u3.shape
    C = RWKV_CHUNK
    n_dbl = int(np.log2(C)) - 1
    outs = [jax.ShapeDtypeStruct((b, t, RWKV_W), F32),
            jax.ShapeDtypeStruct((b, RWKV_HEADS, HEAD_DIM, HEAD_DIM), F32)]
    vec = _full((1, RWKV_W))
    return pl.pallas_call(
        functools.partial(_rwkv_body, nbat=nbat, n_dbl=n_dbl),
        grid=(b // nbat, t // C),
        in_specs=[
            pl.BlockSpec((nbat, C, RWKV_SHIFT_W), lambda bi, c: (bi, c, 0)),
            _full((1, RWKV_SHIFT_W)),
            vec, _full((HEAD_DIM, RWKV_W)), vec, _full((HEAD_DIM, RWKV_W)),
            vec, vec, vec, vec, vec, _full((LANES, LANES)),
        ],
        out_specs=[pl.BlockSpec((nbat, C, RWKV_W), lambda bi, c: (bi, c, 0)),
                   pl.BlockSpec((nbat, RWKV_HEADS, HEAD_DIM, HEAD_DIM), lambda bi, c: (bi, 0, 0, 0))],
        out_shape=outs,
        scratch_shapes=[pltpu.VMEM((nbat * RWKV_HEADS, HEAD_DIM, HEAD_DIM), F32),
                        pltpu.VMEM((nbat, 1, RWKV_SHIFT_W), F32)],
        compiler_params=_cp(("arbitrary", "arbitrary")),
        name="rwkv_prompt",
    )(u3, *rw_args, bd)


def _rwkv_dec_body(u_ref, sh_ref, st_ref, mu_ref, w0_ref, w2t_ref, a0_ref, a2t_ref, kk_ref, ka_ref, rk_ref,
                   lnw_ref, lnb_ref, o_ref, so_ref, a_sc, w_sc, b_sc, k_sc, r_sc, v_sc, y_sc):
    h = pl.program_id(0)
    nh = pl.num_programs(0)
    D = HEAD_DIM

    @pl.when(h == 0)
    def _():
        uT = u_ref[...].T
        pT = sh_ref[...].T
        xT = uT + (pT - uT) * mu_ref[...]
        rT = xT[0:256]
        kT = xT[256:512]
        ww = w0_ref[...] + _nn_x(w2t_ref[...], jnp.tanh(xT[768:832]))
        nw = -ww
        sp = jnp.maximum(nw, 0.0) + jnp.log(1.0 + jnp.exp(-jnp.abs(nw)))
        w_sc[...] = jnp.exp(-jnp.exp(-sp - 0.5))
        aT = _sigmoid(a0_ref[...] + _nn_x(a2t_ref[...], xT[832:896]))
        kk = kT * kk_ref[...]
        for hh in range(RWKV_HEADS):
            blk = kk[hh * D:(hh + 1) * D]
            nrm = jnp.maximum(jnp.sqrt(jnp.sum(blk * blk, axis=0, keepdims=True)), 1e-12)
            a_sc[hh * D:(hh + 1) * D, :] = -(blk / nrm)
        b_sc[...] = -a_sc[...] * aT
        k_sc[...] = kT * (1.0 + (aT - 1.0) * ka_ref[...])
        r_sc[...] = rT
        v_sc[...] = xT[512:768]

    base = pl.multiple_of(h * D, D)
    at = a_sc[pl.ds(base, D), :]
    wt = w_sc[pl.ds(base, D), :]
    bt = b_sc[pl.ds(base, D), :]
    kt = k_sc[pl.ds(base, D), :]
    rt = r_sc[pl.ds(base, D), :]

    def vstep(vi, carry):
        s_v = st_ref[0, vi]
        sa = jnp.sum(s_v * at, axis=0, keepdims=True)
        vv = v_sc[pl.ds(base + vi, 1), :]
        s_n = s_v * wt + sa * bt + vv * kt
        so_ref[0, vi] = s_n
        y_sc[pl.ds(base + vi, 1), :] = jnp.sum(s_n * rt, axis=0, keepdims=True)
        return carry

    lax.fori_loop(0, D, vstep, 0)

    @pl.when(h == nh - 1)
    def _():
        outs = []
        for hh in range(RWKV_HEADS):
            sl = slice(hh * D, (hh + 1) * D)
            y = y_sc[sl, :]
            mean = jnp.mean(y, axis=0, keepdims=True)
            dy = y - mean
            var = jnp.mean(dy * dy, axis=0, keepdims=True)
            yn = dy * lax.rsqrt(var + GN_EPS) * lnw_ref[sl, :] + lnb_ref[sl, :]
            bonus = jnp.sum(r_sc[sl, :] * k_sc[sl, :] * rk_ref[sl, :], axis=0, keepdims=True) * v_sc[sl, :]
            outs.append(yn + bonus)
        o_ref[...] = jnp.concatenate(outs, axis=0).T


def _rwkv_decode(u_s, shift, state_t, rw_cols):
    s = u_s.shape[0]
    D = HEAD_DIM
    col = lambda n: _full((n, 1))
    vec_sc = pltpu.VMEM((RWKV_W, s), F32)
    return pl.pallas_call(
        _rwkv_dec_body,
        grid=(RWKV_HEADS,),
        in_specs=[_full((s, RWKV_SHIFT_W)), _full((s, RWKV_SHIFT_W)),
                  pl.BlockSpec((1, D, D, s), lambda h: (h, 0, 0, 0)),
                  col(RWKV_SHIFT_W), col(RWKV_W), _full((RWKV_W, D)), col(RWKV_W), _full((RWKV_W, D)),
                  col(RWKV_W), col(RWKV_W), col(RWKV_W), col(RWKV_W), col(RWKV_W)],
        out_specs=[_full((s, RWKV_W)), pl.BlockSpec((1, D, D, s), lambda h: (h, 0, 0, 0))],
        out_shape=[jax.ShapeDtypeStruct((s, RWKV_W), F32), jax.ShapeDtypeStruct((RWKV_HEADS, D, D, s), F32)],
        scratch_shapes=[vec_sc] * 7,
        compiler_params=_cp(("arbitrary",)),
        name="rwkv_decode",
    )(u_s, shift, state_t, *rw_cols)


def _memkv_body(m_ref, g_ref, w_ref, kn_ref, bd_ref, o_ref):
    x = m_ref[0]
    ms = jnp.mean(x * x, axis=-1, keepdims=True)
    xn = (x * lax.rsqrt(ms + RMS_EPS) * g_ref[...]).astype(BF16)
    kv = _nn(xn, w_ref[...])
    o_ref[0, :, 0:256] = _head_rms(kv[:, 0:256], kn_ref[...], bd_ref[...])
    o_ref[0, :, 256:512] = kv[:, 256:512]


def _mem_kv(mem, g, w, kn, bd):
    b, m, d = mem.shape
    return pl.pallas_call(
        _memkv_body,
        grid=(b,),
        in_specs=[pl.BlockSpec((1, m, d), lambda i: (i, 0, 0)), _full((1, d)), _full((d, 512)),
                  _full((1, 256)), _full((LANES, LANES))],
        out_specs=pl.BlockSpec((1, m, 512), lambda i: (i, 0, 0)),
        out_shape=jax.ShapeDtypeStruct((b, m, 512), F32),
        compiler_params=_cp(("arbitrary",)),
        name="mem_kv",
    )(mem, g, w, kn, bd)


def _mem_attn_body(qm_ref, kv_ref, o_ref):
    qm = qm_ref[...]
    kvb = kv_ref[0].astype(BF16)
    eye = _eye(LANES)
    heads = []
    for h in range(MEM_HEADS):
        col = h // 2
        kc = kvb[:, col * LANES:(col + 1) * LANES]
        vc = kvb[:, 256 + col * LANES:256 + (col + 1) * LANES]
        s = _nt(kc, qm[:, h * LANES:(h + 1) * LANES])
        m = jnp.max(s, axis=0, keepdims=True)
        p = jnp.exp2(s - m)
        p = p * (1.0 / jnp.sum(p, axis=0, keepdims=True))
        oT = _nn(_nt(eye, vc).astype(BF16), p.astype(BF16))
        heads.append(oT.T)
    lane = lax.broadcasted_iota(jnp.int32, heads[0].shape, 1)
    o_ref[...] = jnp.concatenate([jnp.where(lane < 64, heads[0], heads[1]),
                                  jnp.where(lane < 64, heads[2], heads[3])], axis=1)


def _mem_attn(qm, memkv, nb, rows_per_b, tb):
    n = qm.shape[0]
    nblk = rows_per_b // tb
    m = memkv.shape[1]
    return pl.pallas_call(
        _mem_attn_body,
        grid=(nb, nblk),
        in_specs=[pl.BlockSpec((tb, 512), lambda b, i: (b * nblk + i, 0)),
                  pl.BlockSpec((1, m, 512), lambda b, i: (b, 0, 0))],
        out_specs=pl.BlockSpec((tb, MEM_W), lambda b, i: (b * nblk + i, 0)),
        out_shape=jax.ShapeDtypeStruct((n, MEM_W), F32),
        compiler_params=_cp(("arbitrary", "arbitrary")),
        name="mem_attn",
    )(qm, memkv)


def _out_body(x_ref, on_ref, or_ref, om_ref, zs_ref, w_ref, y_ref):
    zs = zs_ref[...]
    cat = jnp.concatenate([on_ref[...] * zs[:, 0:512], or_ref[...] * zs[:, 512:768],
                           om_ref[...] * zs[:, 768:1024]], axis=1)
    y_ref[...] = x_ref[...] + _nn(cat.astype(BF16), w_ref[...])


def _out_proj(x2d, o_nsa, o_rw, o_mem, zs, w_out, tb):
    n, d = x2d.shape
    row = lambda i: (i, 0)
    return pl.pallas_call(
        _out_body,
        grid=(n // tb,),
        in_specs=[pl.BlockSpec((tb, d), row), pl.BlockSpec((tb, 512), row), pl.BlockSpec((tb, 256), row),
                  pl.BlockSpec((tb, 256), row), pl.BlockSpec((tb, 1024), row), _full((1024, d))],
        out_specs=pl.BlockSpec((tb, d), row),
        out_shape=jax.ShapeDtypeStruct((n, d), F32),
        compiler_params=_cp(("arbitrary",)),
        name="out_proj",
    )(x2d, o_nsa, o_rw, o_mem, zs, w_out)


PAGES_PER_STEP = 16
SKEW = CMP_STRIDE + 1


def _query_rows(qrow, g):
    rows = lax.broadcasted_iota(jnp.int32, (16, LANES), 0)
    qg = jnp.zeros((16, LANES), F32)
    for h in range(4):
        piece = qrow[:, (4 * g + h) * LANES:(4 * g + h + 1) * LANES].astype(F32)
        qg = jnp.where(rows == h, piece, qg)
    return qg.astype(BF16)


def _dec_a_body(pt_ref, *refs, n_ch, n_sb, n_sbp, p_len):
    pages = refs[:PAGES_PER_STEP]
    (q_ref, pelo_ref, pehi_ref, w1lo_ref, w1hi_ref, b1_ref, w2_ref, b2_ref, ckn_ref, cos_ref, sin_ref,
     bd_ref, ov_ref, oc_out, idx_out, xk, xv) = refs[PAGES_PER_STEP:]
    j = pl.program_id(1)
    cpp = PAGE_SIZE // CMP_STRIDE
    for i in range(PAGES_PER_STEP):
        r0 = pl.multiple_of((j * PAGES_PER_STEP + i) * cpp * SKEW, 8)
        for e, dst in ((0, xk), (1, xv)):
            tt = pages[i][0, e].T
            for c in range(cpp):
                dst[pl.ds(r0 + c * SKEW, CMP_STRIDE), :] = tt[c * CMP_STRIDE:(c + 1) * CMP_STRIDE, :]

    @pl.when(j == pl.num_programs(1) - 1)
    def _():
        lo = jnp.zeros((n_ch, 256), F32)
        hi = jnp.zeros((n_ch, 256), F32)
        for p in range(CMP_STRIDE):
            xp = jnp.concatenate([xk[pl.ds(p, n_ch, stride=SKEW), :], xv[pl.ds(p, n_ch, stride=SKEW), :]], axis=1)
            lo = lo + _nn((xp + pelo_ref[p:p + 1, :]).astype(BF16), w1lo_ref[p])
            hi = hi + _nn((xp + pehi_ref[p:p + 1, :]).astype(BF16), w1hi_ref[p])
        ck, cv = _compress_tail(lo, hi, n_ch, b1_ref, w2_ref, b2_ref, ckn_ref, cos_ref, sin_ref, bd_ref)
        ckb = ck.astype(BF16)
        cvb = cv.astype(BF16)
        qrow = q_ref[0]
        cend = lax.broadcasted_iota(jnp.int32, (1, n_ch), 1) * CMP_STRIDE + (CMP_BLK - 1)
        valid = cend <= p_len
        ocs = []
        imps = []
        for g in range(NSA_KV_HEADS):
            s = _nt(_query_rows(qrow, g), ckb)
            sm = jnp.where(valid, s, NEG)
            m = jnp.max(sm, axis=1, keepdims=True)
            p = jnp.where(valid, jnp.exp2(sm - m), 0.0)
            p = p * (1.0 / jnp.maximum(jnp.sum(p, axis=1, keepdims=True), 1e-30))
            ocs.append(_nn(p.astype(BF16), cvb))
            imps.append(p[0:1] + p[1:2] + p[2:3] + p[3:4])
        oc_out[0] = jnp.concatenate(ocs, axis=1)
        pz = jnp.concatenate(imps + [jnp.zeros((6, n_ch), F32)], axis=0)
        ph, plo = _split2(pz)
        imp = _nn(ph, ov_ref[...]) + _nn(plo, ov_ref[...])
        jj = lax.broadcasted_iota(jnp.int32, (8, n_sbp), 1)
        cur = p_len // SEL_BLK
        forced = (jj == 0) | (jj == cur) | (jj == cur - 1)
        score = jnp.where(jj < n_sb, jnp.where(forced, jnp.inf, imp), -jnp.inf)
        lane = lax.broadcasted_iota(jnp.int32, (8, LANES), 1)
        idx = jnp.zeros((8, LANES), jnp.int32)
        for n in range(N_SEL):
            mx = jnp.max(score, axis=1, keepdims=True)
            first = jnp.min(jnp.where(score == mx, jj, n_sbp), axis=1, keepdims=True)
            idx = jnp.where(lane == n, first, idx)
            score = jnp.where(jj == first, -jnp.inf, score)
        idx_out[0] = idx


def _decode_a(pt, pool_t, q3, cmp_args, ov, s, p_len):
    n_pages = p_len // PAGE_SIZE
    n_ch = p_len // CMP_STRIDE
    n_sb = p_len // SEL_BLK + 1
    n_sbp = ov.shape[1]
    steps = n_pages // PAGES_PER_STEP

    def page_spec(i):
        return pl.BlockSpec((1, 2, LANES, PAGE_SIZE),
                            lambda si, j, pt_ref: (pt_ref[si, j * PAGES_PER_STEP + i], 0, 0, 0))

    full = lambda shape: pl.BlockSpec(shape, lambda si, j, pt_ref: (0,) * len(shape))
    cmp_specs = [full((CMP_STRIDE, 256)), full((CMP_STRIDE, 256)), full((CMP_STRIDE, 256, 256)),
                 full((CMP_STRIDE, 256, 256)), full((1, 256)), full((256, 256)), full((1, 256)),
                 full((1, LANES)), full((n_ch, LANES)), full((n_ch, LANES)), full((LANES, LANES))]
    grid_spec = pltpu.PrefetchScalarGridSpec(
        num_scalar_prefetch=1,
        grid=(s, steps),
        in_specs=[page_spec(i) for i in range(PAGES_PER_STEP)]
        + [pl.BlockSpec((1, 1, 1024), lambda si, j, pt_ref: (si, 0, 0))] + cmp_specs + [full(ov.shape)],
        out_specs=[pl.BlockSpec((1, 16, 256), lambda si, j, pt_ref: (si, 0, 0)),
                   pl.BlockSpec((1, 8, LANES), lambda si, j, pt_ref: (si, 0, 0))],
        scratch_shapes=[pltpu.VMEM((n_ch * SKEW, LANES), F32), pltpu.VMEM((n_ch * SKEW, LANES), F32)],
    )
    return pl.pallas_call(
        functools.partial(_dec_a_body, n_ch=n_ch, n_sb=n_sb, n_sbp=n_sbp, p_len=p_len),
        grid_spec=grid_spec,
        out_shape=[jax.ShapeDtypeStruct((s, 16, 256), F32), jax.ShapeDtypeStruct((s, 8, LANES), jnp.int32)],
        compiler_params=_cp(("arbitrary", "arbitrary")),
        name="decode_select",
    )(pt, *([pool_t] * PAGES_PER_STEP), q3, *cmp_args, ov)


def _dec_b_body(pt_ref, idx_ref, pool_ref, q_ref, g_ref, oc_ref, new4_ref, neww_ref, win_ref, o_out, win_out,
                buf, sem, kall, vall, *, n_blk):
    nsel = NSA_KV_HEADS * N_SEL
    si = pl.program_id(0)
    n_seq = pl.num_programs(0)
    slot = lax.rem(si, 2)

    def page_copy(s, n, sl):
        jb = jnp.minimum(idx_ref[s, n], n_blk - 1)
        page = pt_ref[s, lax.shift_right_logical(jb, 1)]
        return pltpu.make_async_copy(pool_ref.at[page, pl.ds(2, 2)], buf.at[sl, n], sem.at[sl])

    @pl.when(si == 0)
    def _():
        for n in range(nsel):
            page_copy(0, n, 0).start()

    @pl.when(si + 1 < n_seq)
    def _():
        for n in range(nsel):
            page_copy(si + 1, n, 1 - slot).start()

    for n in range(nsel):
        page_copy(si, n, slot).wait()

    qrow = q_ref[0]
    gates = g_ref[0]
    oc = oc_ref[0]
    new4 = new4_ref[0]
    neww = neww_ref[0]

    wlen = win_ref.shape[3]
    r0 = lax.broadcasted_iota(jnp.int32, (LANES, LANES), 0)
    r1 = lax.broadcasted_iota(jnp.int32, (LANES, LANES), 1)
    lane_w = lax.broadcasted_iota(jnp.int32, (LANES, wlen), 1)
    new_t = []
    for c in range(2):
        col = jnp.sum(jnp.where(r0 == r1, neww[:, c * LANES:(c + 1) * LANES], 0.0), axis=1, keepdims=True)
        shifted = pltpu.roll(win_ref[0, c], wlen - 1, axis=1)
        nt = jnp.where(lane_w == wlen - 1, col, shifted)
        win_out[0, c] = nt
        new_t.append(nt.astype(BF16))
    kwT, vwT = new_t

    ksel_new = new4[:, 256:384].astype(BF16).astype(F32)
    vsel_new = new4[:, 384:512].astype(BF16).astype(F32)
    nk = N_SEL * PAGE_SIZE
    lane_k = lax.broadcasted_iota(jnp.int32, (16, nk), 1)

    heads = []
    for g in range(NSA_KV_HEADS):
        qgb = _query_rows(qrow, g)
        valid = lane_k < 0
        for n in range(N_SEL):
            kall[:, n * PAGE_SIZE:(n + 1) * PAGE_SIZE] = buf[slot, g * N_SEL + n, 0].astype(BF16)
            vall[:, n * PAGE_SIZE:(n + 1) * PAGE_SIZE] = buf[slot, g * N_SEL + n, 1].astype(BF16)
            jb = idx_ref[si, g * N_SEL + n]
            half = jb % 2
            valid = valid | ((jb < n_blk) & (lax.shift_right_logical(lane_k, 7) == n)
                             & ((lax.shift_right_logical(lane_k, 6) & 1) == half))
        s = _nn(qgb, kall[...])
        s_new = jnp.sum(qgb.astype(F32) * ksel_new, axis=1, keepdims=True)
        sm = jnp.where(valid, s, NEG)
        m = jnp.maximum(jnp.max(sm, axis=1, keepdims=True), s_new)
        p = jnp.where(valid, jnp.exp2(sm - m), 0.0)
        p_new = jnp.exp2(s_new - m)
        l = jnp.sum(p, axis=1, keepdims=True) + p_new
        o_s = (_nt(p.astype(BF16), vall[...]) + p_new.astype(BF16).astype(F32) * vsel_new) * (1.0 / l)

        sw = _nn(qgb, kwT)
        mw = jnp.max(sw, axis=1, keepdims=True)
        pw = jnp.exp2(sw - mw)
        o_w = _nt(pw.astype(BF16), vwT) * (1.0 / jnp.sum(pw, axis=1, keepdims=True))

        o_c = oc[:, g * LANES:(g + 1) * LANES]
        for h in range(4):
            hd = 4 * g + h
            mix = (gates[:, 3 * hd:3 * hd + 1] * o_c[h:h + 1, :] + gates[:, 3 * hd + 1:3 * hd + 2] * o_s[h:h + 1, :]
                   + gates[:, 3 * hd + 2:3 * hd + 3] * o_w[h:h + 1, :])
            if hd % 2 != g:
                mix = pltpu.roll(mix, 64, axis=1)
            heads.append(mix)
    lane1 = lax.broadcasted_iota(jnp.int32, (1, LANES), 1)
    cols = [jnp.where(lane1 < 64, heads[2 * c], heads[2 * c + 1]) for c in range(4)]
    o_out[0] = jnp.concatenate(cols, axis=1)


def _decode_b(pt, idx32, pool_t, q3, g3, oc, new4, neww, win_t, s, p_len):
    n_blk = p_len // SEL_BLK
    nsel = NSA_KV_HEADS * N_SEL
    wlen = win_t.shape[3]

    per = lambda shape: pl.BlockSpec(shape, lambda si, pt_ref, idx_ref: (si,) + (0,) * (len(shape) - 1))
    grid_spec = pltpu.PrefetchScalarGridSpec(
        num_scalar_prefetch=2,
        grid=(s,),
        in_specs=[pl.BlockSpec(memory_space=pl.ANY),
                  per((1, 1, 1024)), per((1, 1, LANES)), per((1, 16, 256)), per((1, 1, 512)), per((1, 1, 256)),
                  per((1, 2, LANES, wlen))],
        out_specs=[per((1, 1, 512)), per((1, 2, LANES, wlen))],
        scratch_shapes=[pltpu.VMEM((2, nsel, 2, LANES, PAGE_SIZE), F32), pltpu.SemaphoreType.DMA((2,)),
                        pltpu.VMEM((LANES, N_SEL * PAGE_SIZE), BF16), pltpu.VMEM((LANES, N_SEL * PAGE_SIZE), BF16)],
    )
    return pl.pallas_call(
        functools.partial(_dec_b_body, n_blk=n_blk),
        grid_spec=grid_spec,
        out_shape=[jax.ShapeDtypeStruct((s, 1, 512), F32), jax.ShapeDtypeStruct((s, 2, LANES, wlen), F32)],
        compiler_params=_cp(("arbitrary",)),
        name="decode_attend",
    )(pt, idx32, pool_t, q3, g3, oc, new4, neww, win_t)


def _mem_dec_body(qm_ref, kv_ref, o_ref):
    qrow = qm_ref[0]
    kT = kv_ref[0, 0].astype(BF16)
    vT = kv_ref[0, 1].astype(BF16)
    rows = lax.broadcasted_iota(jnp.int32, (16, 256), 0)
    lanes = lax.broadcasted_iota(jnp.int32, (16, 256), 1)
    qr = jnp.zeros((16, 256), F32)
    for h in range(MEM_HEADS):
        piece = qrow[:, h * LANES:(h + 1) * LANES].astype(F32)
        col = h // 2
        wide = jnp.concatenate([piece if c == col else jnp.zeros_like(piece) for c in range(2)], axis=1)
        qr = jnp.where(rows == h, wide, qr)
    s = _nn(qr.astype(BF16), kT)
    m = jnp.max(s, axis=1, keepdims=True)
    p = jnp.exp2(s - m)
    p = p * (1.0 / jnp.sum(p, axis=1, keepdims=True))
    of = _nt(p.astype(BF16), vT)
    keep = rows == lax.shift_right_logical(lanes, 6)
    o_ref[0] = jnp.sum(jnp.where(keep, of, 0.0), axis=0, keepdims=True)


def _mem_decode(qm3, mem_t):
    s, _, hd, m = mem_t.shape
    per = lambda shape: pl.BlockSpec(shape, lambda i: (i,) + (0,) * (len(shape) - 1))
    return pl.pallas_call(
        _mem_dec_body,
        grid=(s,),
        in_specs=[per((1, 1, 512)), per((1, 2, hd, m))],
        out_specs=per((1, 1, MEM_W)),
        out_shape=jax.ShapeDtypeStruct((s, 1, MEM_W), F32),
        compiler_params=_cp(("arbitrary",)),
        name="mem_decode",
    )(qm3, mem_t)


def _prep_w_in(w_in):
    d = w_in.shape[0]
    q_w, kv_w, g_w, zn_w, u_w, zr_w, qm_w, zm_w = jnp.split(
        w_in, [int(c) for c in np.cumsum([512, 768, 24, 512, 896, 256, 256])], axis=1)
    zero64 = jnp.zeros((d, HEAD_DIM), w_in.dtype)
    q_slots = []
    for hd in range(NSA_HEADS):
        wq = q_w[:, hd * HEAD_DIM:(hd + 1) * HEAD_DIM]
        q_slots += [wq, zero64] if hd // NSA_HPG == 0 else [zero64, wq]
    qm_slots = []
    for h in range(MEM_HEADS):
        wq = qm_w[:, h * HEAD_DIM:(h + 1) * HEAD_DIM]
        qm_slots += [wq, zero64] if h % 2 == 0 else [zero64, wq]
    g_pad = jnp.pad(g_w, ((0, 0), (0, LANES - g_w.shape[1])))
    w_p = jnp.concatenate(q_slots + [kv_w, zn_w, zr_w, zm_w, u_w] + qm_slots + [g_pad], axis=1)
    return w_p.astype(BF16)


def _rope_table(pos):
    half = HEAD_DIM // 2
    inv = ROPE_THETA ** (-2.0 * jnp.arange(half, dtype=F32) / HEAD_DIM)
    ang = pos.astype(F32)[:, None] * inv[None, :]
    cos = jnp.cos(ang)
    sin = jnp.sin(ang)
    return jnp.tile(jnp.concatenate([cos, cos], axis=1), (1, 2)), jnp.tile(jnp.concatenate([-sin, sin], axis=1), (1, 2))


def _block_diag(blocks):
    n = len(blocks)
    r, c = blocks[0].shape
    out = jnp.zeros((n * r, n * c), blocks[0].dtype)
    for i, blk in enumerate(blocks):
        out = out.at[i * r:(i + 1) * r, i * c:(i + 1) * c].set(blk)
    return out


def _prep_compress(cmp_pe, cmp_w1, cmp_b1, cmp_w2, cmp_b2, ck_norm, cend):
    eg = [(0, 0), (0, 1), (1, 0), (1, 1)]
    pelo = jnp.concatenate([cmp_pe[e, :CMP_STRIDE] for e, _ in eg], axis=1)
    pehi = jnp.concatenate([cmp_pe[e, CMP_STRIDE:] for e, _ in eg], axis=1)
    w1lo = jnp.stack([_block_diag([cmp_w1[e, p] for e, _ in eg]) for p in range(CMP_STRIDE)]).astype(BF16)
    w1hi = jnp.stack([_block_diag([cmp_w1[e, CMP_STRIDE + p] for e, _ in eg]) for p in range(CMP_STRIDE)]).astype(BF16)
    b1 = jnp.concatenate([cmp_b1[e] for e, _ in eg])[None, :]
    w2 = _block_diag([cmp_w2[e] for e, _ in eg]).astype(BF16)
    b2 = jnp.concatenate([cmp_b2[e] for e, _ in eg])[None, :]
    ckn = jnp.tile(ck_norm, 2)[None, :]
    cos_c, sin_c = _rope_table(cend)
    return [pelo, pehi, w1lo, w1hi, b1, w2, b2, ckn, cos_c, sin_c]


def _overlap(n_cb, n_sb):
    cstart = np.arange(n_cb)[:, None] * CMP_STRIDE
    sstart = np.arange(n_sb)[None, :] * SEL_BLK
    ov = np.clip(np.minimum(cstart + CMP_BLK, sstart + SEL_BLK) - np.maximum(cstart, sstart), 0, None)
    return ov.astype(np.float32) / CMP_BLK


def _pad_to(a, rows, cols):
    return np.pad(a, ((0, rows - a.shape[0]), (0, cols - a.shape[1])))


def kernel(x_prompt, x_sample, mem_prompt, cache_nsa, cache_win, cache_mem, state_rwkv_shift, state_rwkv_wkv,
           page_table, ln_g, w_in, nsa_q_norm, nsa_k_norm, cmp_pe, cmp_w1, cmp_b1, cmp_w2, cmp_b2, rwkv_mu, rwkv_w0,
           rwkv_w2, rwkv_a0, rwkv_a2, rwkv_k_k, rwkv_k_a, rwkv_r_k, rwkv_ln_w, rwkv_ln_b, mem_norm_g, w_mem_kv,
           mem_q_norm, mem_k_norm, w_out):
    depth = ln_g.shape[0]
    assert depth == 1
    bp, tp, d = x_prompt.shape
    bs, ts, _ = x_sample.shape
    assert ts == 1
    n_pages = page_table.shape[1]
    p_len = n_pages * PAGE_SIZE
    n_pool = cache_nsa.shape[1]
    wlen = cache_win.shape[2]
    mlen = cache_mem.shape[2]
    assert tp % KTILE == 0 and tp >= WINDOW + QBLK and n_pages % PAGES_PER_STEP == 0 and wlen == WINDOW
    l = 0

    bd = jnp.asarray(np.kron(np.eye(2), np.ones((HEAD_DIM, HEAD_DIM))), BF16)
    w_p = _prep_w_in(w_in[l])
    lng = ln_g[l][None, :]
    qn = jnp.tile(nsa_q_norm[l], 16)[None, :]
    kn = jnp.tile(nsa_k_norm[l], (1, 2))
    mqn = jnp.tile(mem_q_norm[l], 8)[None, :]
    w_out_b = w_out[l].astype(BF16)
    rw_vecs = [rwkv_mu[l], rwkv_w0[l], rwkv_a0[l], rwkv_k_k[l], rwkv_k_a[l], rwkv_r_k[l].reshape(RWKV_W),
               rwkv_ln_w[l], rwkv_ln_b[l]]
    mu, w0, a0, kkp, kap, rkp, lnw, lnb = rw_vecs
    rw_args = [mu[None, :], w0[None, :], rwkv_w2[l], a0[None, :], rwkv_a2[l], kkp[None, :], kap[None, :],
               rkp[None, :], lnw[None, :], lnb[None, :]]
    rw_cols = [mu[:, None], w0[:, None], rwkv_w2[l].T, a0[:, None], rwkv_a2[l].T, kkp[:, None], kap[:, None],
               rkp[:, None], lnw[:, None], lnb[:, None]]

    tb = 512
    xp2 = x_prompt.reshape(bp * tp, d)
    cos_p, sin_p = _rope_table(jnp.arange(tp))
    q, rows4, rows4t, rows_w, kvb, vt, zs, u_rw, qm, gates = _project(
        xp2, bp, tp, lng, w_p, cos_p, sin_p, tp // tb, qn, kn, mqn, bd, tb)

    n_ch = tp // CMP_STRIDE
    cend_p = jnp.arange(n_ch) * CMP_STRIDE + CMP_BLK - 1
    cmp_args_p = _prep_compress(cmp_pe[l], cmp_w1[l], cmp_b1[l], cmp_w2[l], cmp_b2[l], nsa_k_norm[l, 0], cend_p) + [bd]
    ck, cv = _compress_prompt(rows4, cmp_args_p, bp, tp)
    n_sb = tp // SEL_BLK
    n_sbp = -(-n_sb // LANES) * LANES
    ov = _overlap(n_ch - 1, n_sb)
    ovt = jnp.asarray(_pad_to(ov.T, n_sbp, n_ch), BF16)
    o_nsa = _nsa_prompt(q, gates, kvb, vt, ck, cv, ovt, bp, tp)

    nbat = max(n for n in (8, 4, 2, 1) if bp % n == 0)
    o_rw, wkv_p = _rwkv_prompt(u_rw.reshape(bp, tp, RWKV_SHIFT_W), rw_args, bd, nbat)
    o_rw = o_rw.reshape(bp * tp, RWKV_W)

    kmn = jnp.tile(mem_k_norm[l], 4)[None, :]
    memkv_p = _mem_kv(mem_prompt, mem_norm_g[l][None, :], w_mem_kv[l].astype(BF16), kmn, bd)
    o_mem = _mem_attn(qm, memkv_p, bp, tp, 512)
    y_prompt = _out_proj(xp2, o_nsa, o_rw, o_mem, zs, w_out_b, tb).reshape(bp, tp, d)

    nsa_rows_prompt = rows4t.reshape(bp, 4, NSA_KV_HEADS, HEAD_DIM, tp).transpose(0, 4, 1, 2, 3)[None]
    win_p = min(WINDOW, tp)
    win_prompt = rows_w.reshape(bp, tp, 2, NSA_KV_HEADS, HEAD_DIM)[None, :, tp - win_p:]
    shift_prompt = u_rw.reshape(bp, tp, RWKV_SHIFT_W)[None, :, -1]
    mem_kv_prompt = memkv_p.reshape(1, bp, mlen, 2, MEM_HEADS, HEAD_DIM)

    xs2 = x_sample.reshape(bs, d)
    cos_s, sin_s = _rope_table(jnp.full((bs,), p_len))
    q_s, rows4_s, rows4t_s, rows_w_s, _, _, zs_s, u_s, qm_s, gates_s = _project(
        xs2, 1, bs, lng, w_p, cos_s, sin_s, 1, qn, kn, mqn, bd, bs)

    n_ch_s = p_len // CMP_STRIDE
    cend_s = jnp.arange(n_ch_s) * CMP_STRIDE + CMP_BLK - 1
    cmp_args_s = _prep_compress(cmp_pe[l], cmp_w1[l], cmp_b1[l], cmp_w2[l], cmp_b2[l], nsa_k_norm[l, 0], cend_s) + [bd]
    n_sb_s = p_len // SEL_BLK + 1
    n_sbp_s = -(-n_sb_s // LANES) * LANES
    ov_s = jnp.asarray(_pad_to(_overlap(n_ch_s - 1, n_sb_s), n_ch_s, n_sbp_s), BF16)
    pool_t = cache_nsa[l].transpose(0, 2, 3, 4, 1).reshape(n_pool, 4, NSA_KV_HEADS * HEAD_DIM, PAGE_SIZE)
    oc, idx = _decode_a(page_table, pool_t, q_s.reshape(bs, 1, 1024), cmp_args_s, ov_s, bs, p_len)
    idx32 = idx[:, 0:NSA_KV_HEADS, 0:N_SEL].reshape(bs, NSA_KV_HEADS * N_SEL)
    win_t = cache_win[l].transpose(0, 2, 3, 4, 1).reshape(bs, 2, NSA_KV_HEADS * HEAD_DIM, wlen)
    o_nsa_s, win_s = _decode_b(page_table, idx32, pool_t, q_s.reshape(bs, 1, 1024), gates_s.reshape(bs, 1, LANES), oc,
                               rows4_s.reshape(bs, 1, 512), rows_w_s.reshape(bs, 1, 256), win_t, bs, p_len)

    state_t = state_rwkv_wkv[l].transpose(1, 2, 3, 0)
    o_rw_s, wkv_s_t = _rwkv_decode(u_s, state_rwkv_shift[l], state_t, rw_cols)

    mem_t = cache_mem[l].transpose(0, 2, 3, 4, 1).reshape(bs, 2, MEM_HEADS * HEAD_DIM, mlen)
    o_mem_s = _mem_decode(qm_s.reshape(bs, 1, 512), mem_t).reshape(bs, MEM_W)
    y_sample = _out_proj(xs2, o_nsa_s.reshape(bs, 512), o_rw_s, o_mem_s, zs_s, w_out_b, bs).reshape(bs, 1, d)

    nsa_rows_sample = rows4t_s.reshape(4, NSA_KV_HEADS, HEAD_DIM, bs).transpose(3, 0, 1, 2)[None, :, None]
    win_sample = win_s.reshape(bs, 2, NSA_KV_HEADS, HEAD_DIM, wlen).transpose(0, 4, 1, 2, 3)[None]
    shift_sample = u_s[None]
    wkv_sample = wkv_s_t.transpose(3, 0, 1, 2)[None]
    return (y_prompt, y_sample, nsa_rows_prompt, win_prompt, shift_prompt, wkv_p[None], mem_kv_prompt,
            nsa_rows_sample, win_sample, shift_sample, wkv_sample)
```
